```python
import jax, jax.numpy as jnp
from jax import lax
import numpy as np

D_MODEL = 1024
BATCH = 2
SEQ = 8192
DEPTH = 2

D_MIX = D_MODEL
CONV_WIDTH = D_MIX // 4
CONV_GROUPS = 4
CONV_K = 31
MLA_HEADS = 8
MLA_NOPE = 64
MLA_ROPE = 32
MLA_QK = MLA_NOPE + MLA_ROPE
MLA_V = 64
MLA_WIDTH = MLA_HEADS * MLA_V
Q_LORA = 768
KV_LORA = 256
ROPE_THETA = 10000.0
Q_BLOCK = 128
SG_WIDTH = D_MIX - CONV_WIDTH - MLA_WIDTH
SG_HEADS = 4
SG_HEAD_DIM = SG_WIDTH // SG_HEADS
SG_CHUNK = 128
IN_SIZES = (CONV_WIDTH, CONV_WIDTH, CONV_WIDTH,
            Q_LORA, KV_LORA, MLA_ROPE, MLA_WIDTH,
            SG_WIDTH, SG_WIDTH, SG_WIDTH)
IN_COLS = 3 * CONV_WIDTH + Q_LORA + KV_LORA + MLA_ROPE + MLA_WIDTH + 3 * SG_WIDTH
EPS = 1e-6

kernel_name = 'hybrid_conv_mla_sgu_parallel_heads'


def _rms_norm(x, g):
    xf = x.astype(jnp.float32)
    y = xf * lax.rsqrt(jnp.mean(xf * xf, axis=-1, keepdims=True) + EPS)
    return (y * g.astype(jnp.float32)).astype(x.dtype)


def _layer_norm(x, g, b):
    xf = x.astype(jnp.float32)
    mu = jnp.mean(xf, axis=-1, keepdims=True)
    var = jnp.mean(jnp.square(xf - mu), axis=-1, keepdims=True)
    y = (xf - mu) * lax.rsqrt(var + EPS) * g.astype(jnp.float32) + b.astype(jnp.float32)
    return y.astype(x.dtype)


def _rope_tables(seq):
    half = MLA_ROPE // 2
    inv_freq = ROPE_THETA ** (-jnp.arange(half, dtype=jnp.float32) / half)
    ang = jnp.arange(seq, dtype=jnp.float32)[:, None] * inv_freq[None, :]
    return jnp.cos(ang), jnp.sin(ang)


def _apply_rope(x, cos, sin):
    half = MLA_ROPE // 2
    c = cos[None, :, None, :].astype(x.dtype)
    s = sin[None, :, None, :].astype(x.dtype)
    x1, x2 = x[..., :half], x[..., half:]
    return jnp.concatenate([x1 * c - x2 * s, x1 * s + x2 * c], axis=-1)


def _conv_branch(a, a_glu, conv_w, conv_b, ln_g, ln_b, pw_w, pw_b):
    y = a * jax.nn.sigmoid(a_glu)
    y = lax.conv_general_dilated(
        y, conv_w[:, None, :], window_strides=(1,),
        padding=[(CONV_K - 1, 0)],
        dimension_numbers=('NWC', 'WIO', 'NWC'),
        feature_group_count=CONV_WIDTH) + conv_b
    y = jax.nn.silu(_layer_norm(y, ln_g, ln_b))
    return y @ pw_w + pw_b


def _causal_attention(q, k, v):
    B, S, H, Dq = q.shape
    nb = S // Q_BLOCK
    scale = Dq ** -0.5
    q_blocks = jnp.moveaxis(q.reshape(B, nb, Q_BLOCK, H, Dq), 1, 0)
    k_pos = jnp.arange(S)

    def block(args):
        qb, i = args
        q_pos = i * Q_BLOCK + jnp.arange(Q_BLOCK)
        s = jnp.einsum('bqhd,bkhd->bhqk', qb, k).astype(jnp.float32) * scale
        s = jnp.where(k_pos[None, :] <= q_pos[:, None], s, -jnp.inf)
        p = jax.nn.softmax(s, axis=-1).astype(v.dtype)
        return jnp.einsum('bhqk,bkhd->bqhd', p, v)

    out = lax.map(block, (q_blocks, jnp.arange(nb)))
    return jnp.moveaxis(out, 0, 1).reshape(B, S, H * v.shape[-1])


def _mla_branch(c_q, c_kv, k_rope, q_norm_g, w_uq, kv_norm_g, w_ukv,
                qk_q_g, qk_k_g, cos, sin):
    B, S, _ = c_q.shape
    q = (_rms_norm(c_q, q_norm_g) @ w_uq).reshape(B, S, MLA_HEADS, MLA_QK)
    kv = (_rms_norm(c_kv, kv_norm_g) @ w_ukv).reshape(B, S, MLA_HEADS, MLA_NOPE + MLA_V)
    k_nope, v = kv[..., :MLA_NOPE], kv[..., MLA_NOPE:]
    k_r = jnp.broadcast_to(k_rope[:, :, None, :], (B, S, MLA_HEADS, MLA_ROPE))
    k = jnp.concatenate([k_nope, k_r], axis=-1)
    q = _rms_norm(q, qk_q_g)
    k = _rms_norm(k, qk_k_g)
    q = jnp.concatenate([q[..., :MLA_NOPE], _apply_rope(q[..., MLA_NOPE:], cos, sin)], axis=-1)
    k = jnp.concatenate([k[..., :MLA_NOPE], _apply_rope(k[..., MLA_NOPE:], cos, sin)], axis=-1)
    return _causal_attention(q, k, v)


def _sgu_branch(u, v, ln_g, ln_b, sg_w, sg_b):
    B, S, _ = u.shape
    nc = S // SG_CHUNK
    u = jax.nn.gelu(u)
    v = _layer_norm(jax.nn.gelu(v), ln_g, ln_b)
    v = v.reshape(B, nc, SG_CHUNK, SG_HEADS, SG_HEAD_DIM)
    mask = jnp.tril(jnp.ones((SG_CHUNK, SG_CHUNK), dtype=bool))
    w = jnp.where(mask[None], sg_w, jnp.zeros_like(sg_w))
    mixed = jnp.einsum('gts,bcsgd->bctgd', w, v) + sg_b.T[None, None, :, :, None]
    return u * mixed.reshape(B, S, SG_WIDTH)


def _layer(x, cos, sin, norm_g, w_in, conv_w, conv_b, conv_ln_g, conv_ln_b,
           conv_pw_w, conv_pw_b, q_norm_g, w_uq, kv_norm_g, w_ukv, qk_q_g, qk_k_g,
           sg_ln_g, sg_ln_b, sg_w, sg_b, branch_norm_g, w_out):
    h = _rms_norm(x, norm_g)
    proj = h @ w_in
    idx = np.cumsum(IN_SIZES)[:-1].tolist()
    (a, a_glu, z_conv, c_q, c_kv, k_rope, z_mla,
     u_sg, v_sg, z_sg) = jnp.split(proj, idx, axis=-1)

    y_conv = _conv_branch(a, a_glu, conv_w, conv_b, conv_ln_g, conv_ln_b,
                          conv_pw_w, conv_pw_b) * jax.nn.silu(z_conv)
    y_mla = _mla_branch(c_q, c_kv, k_rope, q_norm_g, w_uq, kv_norm_g, w_ukv,
                        qk_q_g, qk_k_g, cos, sin) * jax.nn.silu(z_mla)
    y_sg = _sgu_branch(u_sg, v_sg, sg_ln_g, sg_ln_b, sg_w, sg_b) * jax.nn.silu(z_sg)

    g_conv = branch_norm_g[:CONV_WIDTH]
    g_mla = branch_norm_g[CONV_WIDTH:CONV_WIDTH + MLA_WIDTH]
    g_sg = branch_norm_g[CONV_WIDTH + MLA_WIDTH:]
    y = jnp.concatenate([_rms_norm(y_conv, g_conv),
                         _rms_norm(y_mla, g_mla),
                         _rms_norm(y_sg, g_sg)], axis=-1)
    return x + y @ w_out


def setup_inputs(seed: int = 0) -> dict:
    key = jax.random.key(seed)
    ks = jax.random.split(key, 24)
    f32 = jnp.float32

    def nrm(k, shape, scale):
        return jax.random.normal(k, shape, f32) * scale

    def gain(k, shape):
        return 1.0 + 0.02 * jax.random.normal(k, shape, f32)

    L = DEPTH
    return {
        'x': jax.random.normal(ks[0], (BATCH, SEQ, D_MODEL), f32),
        'norm_g': gain(ks[1], (L, D_MODEL)),
        'w_in': nrm(ks[2], (L, D_MODEL, IN_COLS), D_MODEL ** -0.5),
        'conv_w': nrm(ks[3], (L, CONV_K, CONV_WIDTH), CONV_K ** -0.5),
        'conv_b': nrm(ks[4], (L, CONV_WIDTH), 0.01),
        'conv_ln_g': gain(ks[5], (L, CONV_WIDTH)),
        'conv_ln_b': nrm(ks[6], (L, CONV_WIDTH), 0.01),
        'conv_pw_w': nrm(ks[7], (L, CONV_WIDTH, CONV_WIDTH), CONV_WIDTH ** -0.5),
        'conv_pw_b': nrm(ks[8], (L, CONV_WIDTH), 0.01),
        'q_norm_g': gain(ks[9], (L, Q_LORA)),
        'w_uq': nrm(ks[10], (L, Q_LORA, MLA_HEADS * MLA_QK), Q_LORA ** -0.5),
        'kv_norm_g': gain(ks[11], (L, KV_LORA)),
        'w_ukv': nrm(ks[12], (L, KV_LORA, MLA_HEADS * (MLA_NOPE + MLA_V)), KV_LORA ** -0.5),
        'qk_q_g': gain(ks[13], (L, MLA_QK)),
        'qk_k_g': gain(ks[14], (L, MLA_QK)),
        'sg_ln_g': gain(ks[15], (L, SG_WIDTH)),
        'sg_ln_b': nrm(ks[16], (L, SG_WIDTH), 0.01),
        'sg_w': nrm(ks[17], (L, SG_HEADS, SG_CHUNK, SG_CHUNK), SG_CHUNK ** -0.5),
        'sg_b': 1.0 + nrm(ks[18], (L, SG_HEADS, SG_CHUNK), 0.1),
        'branch_norm_g': gain(ks[19], (L, D_MIX)),
        'w_out': nrm(ks[20], (L, D_MIX, D_MODEL), D_MIX ** -0.5),
    }


def reference(x, norm_g, w_in, conv_w, conv_b, conv_ln_g, conv_ln_b, conv_pw_w,
              conv_pw_b, q_norm_g, w_uq, kv_norm_g, w_ukv, qk_q_g, qk_k_g,
              sg_ln_g, sg_ln_b, sg_w, sg_b, branch_norm_g, w_out):
    cos, sin = _rope_tables(x.shape[1])
    for l in range(DEPTH):
        x = _layer(x, cos, sin, norm_g[l], w_in[l], conv_w[l], conv_b[l],
                   conv_ln_g[l], conv_ln_b[l], conv_pw_w[l], conv_pw_b[l],
                   q_norm_g[l], w_uq[l], kv_norm_g[l], w_ukv[l], qk_q_g[l], qk_k_g[l],
                   sg_ln_g[l], sg_ln_b[l], sg_w[l], sg_b[l], branch_norm_g[l], w_out[l])
    return x
```

```python
import functools

import jax
import jax.numpy as jnp
from jax import lax
from jax.experimental import pallas as pl
from jax.experimental.pallas import tpu as pltpu

D_MODEL = 1024
CONV_WIDTH = 256
CONV_K = 31
MLA_HEADS = 8
MLA_NOPE = 64
MLA_ROPE = 32
MLA_QK = MLA_NOPE + MLA_ROPE
MLA_V = 64
MLA_WIDTH = MLA_HEADS * MLA_V
Q_LORA = 768
KV_LORA = 256
ROPE_THETA = 10000.0
SG_WIDTH = 256
SG_HEADS = 4
SG_HEAD_DIM = SG_WIDTH // SG_HEADS
SG_CHUNK = 128
EPS = 1e-6

LANES = 128
HEAD_PAD = LANES
QK_PAD = MLA_HEADS * HEAD_PAD

OFF_A = 0
OFF_GLU = OFF_A + CONV_WIDTH
OFF_ZC = OFF_GLU + CONV_WIDTH
OFF_CQ = OFF_ZC + CONV_WIDTH
OFF_CKV = OFF_CQ + Q_LORA
OFF_KR = OFF_CKV + KV_LORA
OFF_ZM = OFF_KR + HEAD_PAD
OFF_U = OFF_ZM + MLA_WIDTH
OFF_V = OFF_U + SG_WIDTH
OFF_ZS = OFF_V + SG_WIDTH
IN_COLS_PAD = OFF_ZS + SG_WIDTH

CONV_HALO = 32
TM = 256
TQ = 256
TK = 256
HEADS_PER_STEP = 2
NEG_BIG = -1e30
VMEM_LIMIT = 48 * 1024 * 1024

assert TM == TK and TM % SG_CHUNK == 0 and TQ == TK


def _rms(x, g, width):
    ss = jnp.sum(x * x, axis=-1, keepdims=True)
    return x * lax.rsqrt(ss * (1.0 / width) + EPS) * g


def _layer_norm(x, g, b, width):
    mu = jnp.sum(x, axis=-1, keepdims=True) * (1.0 / width)
    xc = x - mu
    var = jnp.sum(xc * xc, axis=-1, keepdims=True) * (1.0 / width)
    return xc * lax.rsqrt(var + EPS) * g + b


def _silu(x):
    return x * jax.nn.sigmoid(x)


def _rope(x, c, s_up, s_dn):
    return x * c + pltpu.roll(x, LANES - MLA_ROPE // 2, 1) * s_up + pltpu.roll(x, MLA_ROPE // 2, 1) * s_dn


def _front_kernel(x_ref, normg_ref, win_ref, convw_ref, convb_ref, clng_ref, clnb_ref,
                  pww_ref, pwb_ref, qng_ref, wuq_ref, kvng_ref, wk_ref, wvt_ref,
                  gq_ref, gk_ref, sglng_ref, sglnb_ref, sgw_ref, sgbm_ref,
                  bgc_ref, bgs_ref, rc_ref, rsu_ref, rsd_ref,
                  ycs_ref, q_ref, k_ref, vt_ref, zg_ref, ybuf_ref):
    f32, bf16 = jnp.float32, jnp.bfloat16
    i = pl.program_id(1)

    x = x_ref[...]
    h = _rms(x, normg_ref[...], D_MODEL).astype(bf16)
    proj = jnp.dot(h, win_ref[...], preferred_element_type=f32)

    @pl.when(i == 0)
    def _():
        ybuf_ref[0:CONV_HALO, :] = jnp.zeros((CONV_HALO, CONV_WIDTH), f32)

    glu = proj[:, OFF_A:OFF_A + CONV_WIDTH] * jax.nn.sigmoid(proj[:, OFF_GLU:OFF_GLU + CONV_WIDTH])
    ybuf_ref[CONV_HALO:CONV_HALO + TM, :] = glu
    conv = jnp.zeros((TM, CONV_WIDTH), f32) + convb_ref[...]
    for j in range(CONV_K):
        start = CONV_HALO - (CONV_K - 1) + j
        conv = conv + ybuf_ref[start:start + TM, :] * convw_ref[j:j + 1, :]
    ybuf_ref[0:CONV_HALO, :] = ybuf_ref[TM:TM + CONV_HALO, :]
    yc = _silu(_layer_norm(conv, clng_ref[...], clnb_ref[...], CONV_WIDTH)).astype(bf16)
    yc = jnp.dot(yc, pww_ref[...], preferred_element_type=f32) + pwb_ref[...]
    yc = yc * _silu(proj[:, OFF_ZC:OFF_ZC + CONV_WIDTH])
    ycs_ref[:, 0:CONV_WIDTH] = _rms(yc, bgc_ref[...], CONV_WIDTH).astype(bf16)

    u = jax.nn.gelu(proj[:, OFF_U:OFF_U + SG_WIDTH])
    v = _layer_norm(jax.nn.gelu(proj[:, OFF_V:OFF_V + SG_WIDTH]),
                    sglng_ref[...], sglnb_ref[...], SG_WIDTH)
    trow = lax.broadcasted_iota(jnp.int32, (SG_CHUNK, SG_CHUNK), 0)
    tcol = lax.broadcasted_iota(jnp.int32, (SG_CHUNK, SG_CHUNK), 1)
    wcat = jnp.concatenate(
        [jnp.where(tcol <= trow, sgw_ref[g], 0.0) for g in range(SG_HEADS)], axis=1).astype(bf16)
    lane = lax.broadcasted_iota(jnp.int32, (SG_CHUNK, SG_WIDTH), 1)
    mixed = []
    for c in range(TM // SG_CHUNK):
        vc = v[c * SG_CHUNK:(c + 1) * SG_CHUNK, :]
        vstack = jnp.concatenate(
            [jnp.where((lane >= g * SG_HEAD_DIM) & (lane < (g + 1) * SG_HEAD_DIM), vc, 0.0)
             for g in range(SG_HEADS)], axis=0).astype(bf16)
        mixed.append(jnp.dot(wcat, vstack, preferred_element_type=f32) + sgbm_ref[...])
    ys = u * jnp.concatenate(mixed, axis=0) * _silu(proj[:, OFF_ZS:OFF_ZS + SG_WIDTH])
    ycs_ref[:, CONV_WIDTH:CONV_WIDTH + SG_WIDTH] = _rms(ys, bgs_ref[...], SG_WIDTH).astype(bf16)

    zg_ref[...] = _silu(proj[:, OFF_ZM:OFF_ZM + MLA_WIDTH]).astype(bf16)
    rc, rsu, rsd = rc_ref[...], rsu_ref[...], rsd_ref[...]

    cq = _rms(proj[:, OFF_CQ:OFF_CQ + Q_LORA], qng_ref[...], Q_LORA).astype(bf16)
    qf = jnp.dot(cq, wuq_ref[...], preferred_element_type=f32)
    gq = gq_ref[...] * (MLA_QK ** -0.5)
    for hd in range(MLA_HEADS):
        qh = qf[:, hd * HEAD_PAD:(hd + 1) * HEAD_PAD]
        qh = _rms(qh, gq, MLA_QK)
        q_ref[:, hd * HEAD_PAD:(hd + 1) * HEAD_PAD] = _rope(qh, rc, rsu, rsd).astype(bf16)

    ckv = _rms(proj[:, OFF_CKV:OFF_CKV + KV_LORA], kvng_ref[...], KV_LORA).astype(bf16)
    kf = jnp.dot(ckv, wk_ref[...], preferred_element_type=f32)
    vt = lax.dot_general(wvt_ref[...], ckv, (((1,), (1,)), ((), ())),
                         preferred_element_type=f32)
    vt_ref[...] = vt.astype(bf16)
    gk = gk_ref[...]
    kr = proj[:, OFF_KR:OFF_KR + HEAD_PAD]
    ss_r = jnp.sum(kr * kr, axis=-1, keepdims=True)
    krr = _rope(kr * gk, rc, rsu, rsd)
    for hd in range(MLA_HEADS):
        kh = kf[:, hd * HEAD_PAD:(hd + 1) * HEAD_PAD]
        ss = jnp.sum(kh * kh, axis=-1, keepdims=True) + ss_r
        k_ref[:, hd * HEAD_PAD:(hd + 1) * HEAD_PAD] = (
            (kh * gk + krr) * lax.rsqrt(ss * (1.0 / MLA_QK) + EPS)).astype(bf16)


def _attn_kernel(q_ref, k_ref, vt_ref, o_ref):
    f32, bf16 = jnp.float32, jnp.bfloat16
    qi = pl.program_id(2)
    qs = [q_ref[:, hh * HEAD_PAD:(hh + 1) * HEAD_PAD] for hh in range(HEADS_PER_STEP)]
    krow = lax.broadcasted_iota(jnp.int32, (TK, TQ), 0)
    qcol = lax.broadcasted_iota(jnp.int32, (TK, TQ), 1)

    def step(j, carry, masked):
        out = []
        for hh in range(HEADS_PER_STEP):
            m, l, acc = carry[hh]
            kb = k_ref[pl.ds(pl.multiple_of(j * TK, TK), TK), hh * HEAD_PAD:(hh + 1) * HEAD_PAD]
            s = lax.dot_general(kb, qs[hh], (((1,), (1,)), ((), ())),
                                preferred_element_type=f32)
            if masked:
                s = jnp.where(krow <= qcol, s, NEG_BIG)
            m_new = jnp.maximum(m, jnp.max(s, axis=0, keepdims=True))
            alpha = jnp.exp(m - m_new)
            p = jnp.exp(s - m_new)
            l = alpha * l + jnp.sum(p, axis=0, keepdims=True)
            vb = vt_ref[j, hh * MLA_V:(hh + 1) * MLA_V, :]
            acc = alpha * acc + jnp.dot(vb, p.astype(bf16), preferred_element_type=f32)
            out.append((m_new, l, acc))
        return tuple(out)

    init = tuple((jnp.full((1, TQ), NEG_BIG, f32), jnp.zeros((1, TQ), f32),
                  jnp.zeros((MLA_V, TQ), f32)) for _ in range(HEADS_PER_STEP))
    carry = lax.fori_loop(0, qi, lambda j, c: step(j, c, False), init)
    carry = step(qi, carry, True)
    for hh in range(HEADS_PER_STEP):
        _, l, acc = carry[hh]
        o_ref[hh * MLA_V:(hh + 1) * MLA_V, :] = acc / l


def _back_kernel(x_ref, ycs_ref, ot_ref, zg_ref, bgm_ref, wout_ref, out_ref):
    f32, bf16 = jnp.float32, jnp.bfloat16
    ym = ot_ref[...].T * zg_ref[...].astype(f32)
    ym = _rms(ym, bgm_ref[...], MLA_WIDTH).astype(bf16)
    ycs = ycs_ref[...]
    y = jnp.concatenate([ycs[:, 0:CONV_WIDTH], ym, ycs[:, CONV_WIDTH:]], axis=1)
    out_ref[...] = x_ref[...] + jnp.dot(y, wout_ref[...], preferred_element_type=f32)


def _rope_lane_tables(seq):
    half = MLA_ROPE // 2
    inv_freq = ROPE_THETA ** (-jnp.arange(half, dtype=jnp.float32) / half)
    ang = jnp.arange(seq, dtype=jnp.float32)[:, None] * inv_freq[None, :]
    cos, sin = jnp.cos(ang), jnp.sin(ang)
    ones = jnp.ones((seq, MLA_NOPE), jnp.float32)
    zpad = jnp.zeros((seq, HEAD_PAD - MLA_QK), jnp.float32)
    zn = jnp.zeros((seq, MLA_NOPE), jnp.float32)
    zh = jnp.zeros((seq, half), jnp.float32)
    c = jnp.concatenate([ones, cos, cos, zpad], axis=1)
    s_up = jnp.concatenate([zn, -sin, zh, zpad], axis=1)
    s_dn = jnp.concatenate([zn, zh, sin, zpad], axis=1)
    return c, s_up, s_dn


def _prep_layer(w_in, w_uq, w_ukv, qk_q_g, qk_k_g, sg_b):
    bf16 = jnp.bfloat16
    zc = lambda n: jnp.zeros((w_in.shape[0], n), w_in.dtype)
    kr0 = 3 * CONV_WIDTH + Q_LORA + KV_LORA
    win = jnp.concatenate(
        [w_in[:, :kr0], zc(MLA_NOPE), w_in[:, kr0:kr0 + MLA_ROPE], zc(HEAD_PAD - MLA_QK),
         w_in[:, kr0 + MLA_ROPE:]], axis=1).astype(bf16)
    wuq = jnp.pad(w_uq.reshape(Q_LORA, MLA_HEADS, MLA_QK),
                  ((0, 0), (0, 0), (0, HEAD_PAD - MLA_QK))).reshape(Q_LORA, QK_PAD).astype(bf16)
    wukv = w_ukv.reshape(KV_LORA, MLA_HEADS, MLA_NOPE + MLA_V)
    wk = jnp.pad(wukv[:, :, :MLA_NOPE],
                 ((0, 0), (0, 0), (0, HEAD_PAD - MLA_NOPE))).reshape(KV_LORA, QK_PAD).astype(bf16)
    wvt = wukv[:, :, MLA_NOPE:].reshape(KV_LORA, MLA_WIDTH).T.astype(bf16)
    pad_g = lambda g: jnp.pad(g, (0, HEAD_PAD - MLA_QK)).reshape(1, HEAD_PAD)
    sgbm = jnp.repeat(sg_b.T, SG_HEAD_DIM, axis=1)
    return win, wuq, wk, wvt, pad_g(qk_q_g), pad_g(qk_k_g), sgbm


def _full(shape):
    return pl.BlockSpec(shape, lambda *_: (0,) * len(shape))


def _layer(x, rope, p):
    B, S, _ = x.shape
    nt = S // TM
    bf16 = jnp.bfloat16
    row = lambda a: a.reshape(1, -1)
    win, wuq, wk, wvt, gq, gk, sgbm = _prep_layer(
        p['w_in'], p['w_uq'], p['w_ukv'], p['qk_q_g'], p['qk_k_g'], p['sg_b'])
    bng = p['branch_norm_g']
    tile = lambda w: pl.BlockSpec((None, TM, w), lambda b, i: (b, i, 0))
    ropespec = pl.BlockSpec((TM, HEAD_PAD), lambda b, i: (i, 0))
    front_in = [
        (x, tile(D_MODEL)),
        (row(p['norm_g']), _full((1, D_MODEL))),
        (win, _full((D_MODEL, IN_COLS_PAD))),
        (p['conv_w'], _full((CONV_K, CONV_WIDTH))),
        (row(p['conv_b']), _full((1, CONV_WIDTH))),
        (row(p['conv_ln_g']), _full((1, CONV_WIDTH))),
        (row(p['conv_ln_b']), _full((1, CONV_WIDTH))),
        (p['conv_pw_w'].astype(bf16), _full((CONV_WIDTH, CONV_WIDTH))),
        (row(p['conv_pw_b']), _full((1, CONV_WIDTH))),
        (row(p['q_norm_g']), _full((1, Q_LORA))),
        (wuq, _full((Q_LORA, QK_PAD))),
        (row(p['kv_norm_g']), _full((1, KV_LORA))),
        (wk, _full((KV_LORA, QK_PAD))),
        (wvt, _full((MLA_WIDTH, KV_LORA))),
        (gq, _full((1, HEAD_PAD))),
        (gk, _full((1, HEAD_PAD))),
        (row(p['sg_ln_g']), _full((1, SG_WIDTH))),
        (row(p['sg_ln_b']), _full((1, SG_WIDTH))),
        (p['sg_w'], _full((SG_HEADS, SG_CHUNK, SG_CHUNK))),
        (sgbm, _full((SG_CHUNK, SG_WIDTH))),
        (row(bng[:CONV_WIDTH]), _full((1, CONV_WIDTH))),
        (row(bng[CONV_WIDTH + MLA_WIDTH:]), _full((1, SG_WIDTH))),
        (rope[0], ropespec), (rope[1], ropespec), (rope[2], ropespec),
    ]
    ycs, q, k, vt, zg = pl.pallas_call(
        _front_kernel,
        grid=(B, nt),
        in_specs=[s for _, s in front_in],
        out_specs=[tile(CONV_WIDTH + SG_WIDTH), tile(QK_PAD), tile(QK_PAD),
                   pl.BlockSpec((None, None, MLA_WIDTH, TM), lambda b, i: (b, i, 0, 0)),
                   tile(MLA_WIDTH)],
        out_shape=[jax.ShapeDtypeStruct((B, S, CONV_WIDTH + SG_WIDTH), bf16),
                   jax.ShapeDtypeStruct((B, S, QK_PAD), bf16),
                   jax.ShapeDtypeStruct((B, S, QK_PAD), bf16),
                   jax.ShapeDtypeStruct((B, nt, MLA_WIDTH, TM), bf16),
                   jax.ShapeDtypeStruct((B, S, MLA_WIDTH), bf16)],
        scratch_shapes=[pltpu.VMEM((CONV_HALO + TM, CONV_WIDTH), jnp.float32)],
        compiler_params=pltpu.CompilerParams(
            dimension_semantics=("arbitrary", "arbitrary"), vmem_limit_bytes=VMEM_LIMIT),
        name="front",
    )(*[a for a, _ in front_in])

    nq, nkv, hps = S // TQ, S // TK, HEADS_PER_STEP
    ot = pl.pallas_call(
        _attn_kernel,
        grid=(B, MLA_HEADS // hps, nq),
        in_specs=[pl.BlockSpec((None, TQ, hps * HEAD_PAD), lambda b, h, i: (b, i, h)),
                  pl.BlockSpec((None, S, hps * HEAD_PAD), lambda b, h, i: (b, 0, h)),
                  pl.BlockSpec((None, nkv, hps * MLA_V, TK), lambda b, h, i: (b, 0, h, 0))],
        out_specs=pl.BlockSpec((None, None, hps * MLA_V, TQ), lambda b, h, i: (b, i, h, 0)),
        out_shape=jax.ShapeDtypeStruct((B, nq, MLA_WIDTH, TQ), jnp.float32),
        compiler_params=pltpu.CompilerParams(
            dimension_semantics=("arbitrary", "arbitrary", "arbitrary"),
            vmem_limit_bytes=VMEM_LIMIT),
        name="attn",
    )(q, k, vt)

    return pl.pallas_call(
        _back_kernel,
        grid=(B, nt),
        in_specs=[tile(D_MODEL), tile(CONV_WIDTH + SG_WIDTH),
                  pl.BlockSpec((None, None, MLA_WIDTH, TM), lambda b, i: (b, i, 0, 0)),
                  tile(MLA_WIDTH), _full((1, MLA_WIDTH)), _full((D_MODEL, D_MODEL))],
        out_specs=tile(D_MODEL),
        out_shape=jax.ShapeDtypeStruct((B, S, D_MODEL), jnp.float32),
        compiler_params=pltpu.CompilerParams(
            dimension_semantics=("arbitrary", "arbitrary"), vmem_limit_bytes=VMEM_LIMIT),
        name="back",
    )(x, ycs, ot, zg, row(bng[CONV_WIDTH:CONV_WIDTH + MLA_WIDTH]), p['w_out'].astype(bf16))


def kernel(x, norm_g, w_in, conv_w, conv_b, conv_ln_g, conv_ln_b, conv_pw_w, conv_pw_b,
           q_norm_g, w_uq, kv_norm_g, w_ukv, qk_q_g, qk_k_g, sg_ln_g, sg_ln_b, sg_w, sg_b,
           branch_norm_g, w_out):
    params = dict(norm_g=norm_g, w_in=w_in, conv_w=conv_w, conv_b=conv_b, conv_ln_g=conv_ln_g,
                  conv_ln_b=conv_ln_b, conv_pw_w=conv_pw_w, conv_pw_b=conv_pw_b,
                  q_norm_g=q_norm_g, w_uq=w_uq, kv_norm_g=kv_norm_g, w_ukv=w_ukv,
                  qk_q_g=qk_q_g, qk_k_g=qk_k_g, sg_ln_g=sg_ln_g, sg_ln_b=sg_ln_b, sg_w=sg_w,
                  sg_b=sg_b, branch_norm_g=branch_norm_g, w_out=w_out)
    rope = _rope_lane_tables(x.shape[1])
    for layer in range(norm_g.shape[0]):
        x = _layer(x, rope, {name: a[layer] for name, a in params.items()})
    return x
```

```python
import functools

import jax
import jax.numpy as jnp
from jax import lax
from jax.experimental import pallas as pl
from jax.experimental.pallas import tpu as pltpu

D_MODEL = 1024
CONV_WIDTH = 256
CONV_K = 31
MLA_HEADS = 8
MLA_NOPE = 64
MLA_ROPE = 32
MLA_QK = MLA_NOPE + MLA_ROPE
MLA_V = 64
MLA_WIDTH = MLA_HEADS * MLA_V
Q_LORA = 768
KV_LORA = 256
ROPE_THETA = 10000.0
SG_WIDTH = 256
SG_HEADS = 4
SG_HEAD_DIM = SG_WIDTH // SG_HEADS
SG_CHUNK = 128
EPS = 1e-6

LANES = 128
HEAD_PAD = LANES
QK_PAD = MLA_HEADS * HEAD_PAD

OFF_A = 0
OFF_GLU = OFF_A + CONV_WIDTH
OFF_ZC = OFF_GLU + CONV_WIDTH
OFF_CQ = OFF_ZC + CONV_WIDTH
OFF_CKV = OFF_CQ + Q_LORA
OFF_KR = OFF_CKV + KV_LORA
OFF_ZM = OFF_KR + HEAD_PAD
OFF_U = OFF_ZM + MLA_WIDTH
OFF_V = OFF_U + SG_WIDTH
OFF_ZS = OFF_V + SG_WIDTH
IN_COLS_PAD = OFF_ZS + SG_WIDTH

CONV_HALO = 32
TM = 256
TQ = 256
TK = 256
NEG_BIG = -1e30
LOG2E = 1.4426950408889634
VMEM_LIMIT = 48 * 1024 * 1024

assert TM == TK and TM % SG_CHUNK == 0 and TQ == TK


def _rms(x, g, width):
    ss = jnp.sum(x * x, axis=-1, keepdims=True)
    return x * lax.rsqrt(ss * (1.0 / width) + EPS) * g


def _layer_norm(x, g, b, width):
    mu = jnp.sum(x, axis=-1, keepdims=True) * (1.0 / width)
    xc = x - mu
    var = jnp.sum(xc * xc, axis=-1, keepdims=True) * (1.0 / width)
    return xc * lax.rsqrt(var + EPS) * g + b


def _silu(x):
    return x * jax.nn.sigmoid(x)


def _rope(x, c, s_up, s_dn):
    return x * c + pltpu.roll(x, LANES - MLA_ROPE // 2, 1) * s_up + pltpu.roll(x, MLA_ROPE // 2, 1) * s_dn


def _front_kernel(x_ref, normg_ref, win_ref, convw_ref, convb_ref, clng_ref, clnb_ref,
                  pww_ref, pwb_ref, qng_ref, wuq_ref, kvng_ref, wk_ref, wvt_ref,
                  gq_ref, gk_ref, sglng_ref, sglnb_ref, sgw_ref, sgbm_ref,
                  bgc_ref, bgs_ref, rc_ref, rsu_ref, rsd_ref,
                  ycs_ref, q_ref, k_ref, vt_ref, zg_ref, ybuf_ref):
    f32, bf16 = jnp.float32, jnp.bfloat16
    i = pl.program_id(1)

    x = x_ref[...]
    h = _rms(x, normg_ref[...], D_MODEL).astype(bf16)
    proj = jnp.dot(h, win_ref[...], preferred_element_type=f32)

    @pl.when(i == 0)
    def _():
        ybuf_ref[0:CONV_HALO, :] = jnp.zeros((CONV_HALO, CONV_WIDTH), f32)

    glu = proj[:, OFF_A:OFF_A + CONV_WIDTH] * jax.nn.sigmoid(proj[:, OFF_GLU:OFF_GLU + CONV_WIDTH])
    ybuf_ref[CONV_HALO:CONV_HALO + TM, :] = glu
    conv = jnp.zeros((TM, CONV_WIDTH), f32) + convb_ref[...]
    for j in range(CONV_K):
        start = CONV_HALO - (CONV_K - 1) + j
        conv = conv + ybuf_ref[start:start + TM, :] * convw_ref[j:j + 1, :]
    ybuf_ref[0:CONV_HALO, :] = ybuf_ref[TM:TM + CONV_HALO, :]
    yc = _silu(_layer_norm(conv, clng_ref[...], clnb_ref[...], CONV_WIDTH)).astype(bf16)
    yc = jnp.dot(yc, pww_ref[...], preferred_element_type=f32) + pwb_ref[...]
    yc = yc * _silu(proj[:, OFF_ZC:OFF_ZC + CONV_WIDTH])
    ycs_ref[:, 0:CONV_WIDTH] = _rms(yc, bgc_ref[...], CONV_WIDTH).astype(bf16)

    u = jax.nn.gelu(proj[:, OFF_U:OFF_U + SG_WIDTH])
    v = _layer_norm(jax.nn.gelu(proj[:, OFF_V:OFF_V + SG_WIDTH]),
                    sglng_ref[...], sglnb_ref[...], SG_WIDTH)
    trow = lax.broadcasted_iota(jnp.int32, (SG_CHUNK, SG_CHUNK), 0)
    tcol = lax.broadcasted_iota(jnp.int32, (SG_CHUNK, SG_CHUNK), 1)
    wcat = jnp.concatenate(
        [jnp.where(tcol <= trow, sgw_ref[g], 0.0) for g in range(SG_HEADS)], axis=1).astype(bf16)
    lane = lax.broadcasted_iota(jnp.int32, (SG_CHUNK, SG_WIDTH), 1)
    mixed = []
    for c in range(TM // SG_CHUNK):
        vc = v[c * SG_CHUNK:(c + 1) * SG_CHUNK, :]
        vstack = jnp.concatenate(
            [jnp.where((lane >= g * SG_HEAD_DIM) & (lane < (g + 1) * SG_HEAD_DIM), vc, 0.0)
             for g in range(SG_HEADS)], axis=0).astype(bf16)
        mixed.append(jnp.dot(wcat, vstack, preferred_element_type=f32) + sgbm_ref[...])
    ys = u * jnp.concatenate(mixed, axis=0) * _silu(proj[:, OFF_ZS:OFF_ZS + SG_WIDTH])
    ycs_ref[:, CONV_WIDTH:CONV_WIDTH + SG_WIDTH] = _rms(ys, bgs_ref[...], SG_WIDTH).astype(bf16)

    zg_ref[...] = _silu(proj[:, OFF_ZM:OFF_ZM + MLA_WIDTH]).astype(bf16)
    rc, rsu, rsd = rc_ref[...], rsu_ref[...], rsd_ref[...]

    cq = _rms(proj[:, OFF_CQ:OFF_CQ + Q_LORA], qng_ref[...], Q_LORA).astype(bf16)
    qf = jnp.dot(cq, wuq_ref[...], preferred_element_type=f32)
    gq = gq_ref[...] * (MLA_QK ** -0.5 * LOG2E)
    for hd in range(MLA_HEADS):
        qh = qf[:, hd * HEAD_PAD:(hd + 1) * HEAD_PAD]
        qh = _rms(qh, gq, MLA_QK)
        q_ref[:, hd * HEAD_PAD:(hd + 1) * HEAD_PAD] = _rope(qh, rc, rsu, rsd).astype(bf16)

    ckv = _rms(proj[:, OFF_CKV:OFF_CKV + KV_LORA], kvng_ref[...], KV_LORA).astype(bf16)
    kf = jnp.dot(ckv, wk_ref[...], preferred_element_type=f32)
    vt = lax.dot_general(wvt_ref[...], ckv, (((1,), (1,)), ((), ())),
                         preferred_element_type=f32)
    vt_ref[...] = vt.astype(bf16)
    gk = gk_ref[...]
    kr = proj[:, OFF_KR:OFF_KR + HEAD_PAD]
    ss_r = jnp.sum(kr * kr, axis=-1, keepdims=True)
    krr = _rope(kr * gk, rc, rsu, rsd)
    for hd in range(MLA_HEADS):
        kh = kf[:, hd * HEAD_PAD:(hd + 1) * HEAD_PAD]
        ss = jnp.sum(kh * kh, axis=-1, keepdims=True) + ss_r
        k_ref[:, hd * HEAD_PAD:(hd + 1) * HEAD_PAD] = (
            (kh * gk + krr) * lax.rsqrt(ss * (1.0 / MLA_QK) + EPS)).astype(bf16)


def _attn_kernel(q_ref, k_ref, vt_ref, o_ref, sa_ref, sb_ref, maxa_ref, maxb_ref, m_ref, l_ref):
    f32, bf16 = jnp.float32, jnp.bfloat16
    qi = pl.program_id(1)
    krow = lax.broadcasted_iota(jnp.int32, (TK, TQ), 0)
    qcol = lax.broadcasted_iota(jnp.int32, (TK, TQ), 1)
    heads = range(MLA_HEADS)
    hslab = lambda hh: slice(hh * HEAD_PAD, (hh + 1) * HEAD_PAD)
    vslab = lambda hh: slice(hh * MLA_V, (hh + 1) * MLA_V)

    def scores(j, s_ref, max_ref, hh):
        kb = k_ref[pl.ds(pl.multiple_of(j * TK, TK), TK), hslab(hh)]
        s = lax.dot_general(kb, q_ref[:, hslab(hh)], (((1,), (1,)), ((), ())),
                            preferred_element_type=f32)
        s_ref[hh] = s
        max_ref[hh] = jnp.max(s, axis=0, keepdims=True)

    def consume(j, s_ref, max_ref, hh, masked):
        s = s_ref[hh]
        if masked:
            s = jnp.where(krow <= qcol, s, NEG_BIG)
            blk_max = jnp.max(s, axis=0, keepdims=True)
        else:
            blk_max = max_ref[hh]
        m = m_ref[hh]
        m_new = jnp.maximum(m, blk_max)
        alpha = jnp.exp2(m - m_new)
        p = jnp.exp2(s - m_new)
        m_ref[hh] = m_new
        l_ref[hh] = alpha * l_ref[hh] + jnp.sum(p, axis=0, keepdims=True)
        pv = jnp.dot(vt_ref[j, vslab(hh), :], p.astype(bf16), preferred_element_type=f32)
        o_ref[vslab(hh), :] = alpha * o_ref[vslab(hh), :] + pv

    def half_step(j, cur, nxt):
        for hh in heads:
            scores(j + 1, nxt[0], nxt[1], hh)
            consume(j, cur[0], cur[1], hh, False)

    buf_a, buf_b = (sa_ref, maxa_ref), (sb_ref, maxb_ref)
    m_ref[...] = jnp.full(m_ref.shape, NEG_BIG, f32)
    l_ref[...] = jnp.zeros(l_ref.shape, f32)
    o_ref[...] = jnp.zeros(o_ref.shape, f32)
    for hh in heads:
        scores(0, sa_ref, maxa_ref, hh)

    def pair(jj, carry):
        half_step(2 * jj, buf_a, buf_b)
        half_step(2 * jj + 1, buf_b, buf_a)
        return carry

    lax.fori_loop(0, qi // 2, pair, 0)

    @pl.when(qi % 2 == 1)
    def _():
        half_step(qi - 1, buf_a, buf_b)
        for hh in heads:
            consume(qi, sb_ref, maxb_ref, hh, True)

    @pl.when(qi % 2 == 0)
    def _():
        for hh in heads:
            consume(qi, sa_ref, maxa_ref, hh, True)

    for hh in heads:
        o_ref[vslab(hh), :] = o_ref[vslab(hh), :] / l_ref[hh]


def _back_kernel(x_ref, ycs_ref, ot_ref, zg_ref, bgm_ref, wout_ref, out_ref):
    f32, bf16 = jnp.float32, jnp.bfloat16
    ym = ot_ref[...].T * zg_ref[...].astype(f32)
    ym = _rms(ym, bgm_ref[...], MLA_WIDTH).astype(bf16)
    ycs = ycs_ref[...]
    y = jnp.concatenate([ycs[:, 0:CONV_WIDTH], ym, ycs[:, CONV_WIDTH:]], axis=1)
    out_ref[...] = x_ref[...] + jnp.dot(y, wout_ref[...], preferred_element_type=f32)


def _rope_lane_tables(seq):
    half = MLA_ROPE // 2
    inv_freq = ROPE_THETA ** (-jnp.arange(half, dtype=jnp.float32) / half)
    ang = jnp.arange(seq, dtype=jnp.float32)[:, None] * inv_freq[None, :]
    cos, sin = jnp.cos(ang), jnp.sin(ang)
    ones = jnp.ones((seq, MLA_NOPE), jnp.float32)
    zpad = jnp.zeros((seq, HEAD_PAD - MLA_QK), jnp.float32)
    zn = jnp.zeros((seq, MLA_NOPE), jnp.float32)
    zh = jnp.zeros((seq, half), jnp.float32)
    c = jnp.concatenate([ones, cos, cos, zpad], axis=1)
    s_up = jnp.concatenate([zn, -sin, zh, zpad], axis=1)
    s_dn = jnp.concatenate([zn, zh, sin, zpad], axis=1)
    return c, s_up, s_dn


def _prep_layer(w_in, w_uq, w_ukv, qk_q_g, qk_k_g, sg_b):
    bf16 = jnp.bfloat16
    zc = lambda n: jnp.zeros((w_in.shape[0], n), w_in.dtype)
    kr0 = 3 * CONV_WIDTH + Q_LORA + KV_LORA
    win = jnp.concatenate(
        [w_in[:, :kr0], zc(MLA_NOPE), w_in[:, kr0:kr0 + MLA_ROPE], zc(HEAD_PAD - MLA_QK),
         w_in[:, kr0 + MLA_ROPE:]], axis=1).astype(bf16)
    wuq = jnp.pad(w_uq.reshape(Q_LORA, MLA_HEADS, MLA_QK),
                  ((0, 0), (0, 0), (0, HEAD_PAD - MLA_QK))).reshape(Q_LORA, QK_PAD).astype(bf16)
    wukv = w_ukv.reshape(KV_LORA, MLA_HEADS, MLA_NOPE + MLA_V)
    wk = jnp.pad(wukv[:, :, :MLA_NOPE],
                 ((0, 0), (0, 0), (0, HEAD_PAD - MLA_NOPE))).reshape(KV_LORA, QK_PAD).astype(bf16)
    wvt = wukv[:, :, MLA_NOPE:].reshape(KV_LORA, MLA_WIDTH).T.astype(bf16)
    pad_g = lambda g: jnp.pad(g, (0, HEAD_PAD - MLA_QK)).reshape(1, HEAD_PAD)
    sgbm = jnp.repeat(sg_b.T, SG_HEAD_DIM, axis=1)
    return win, wuq, wk, wvt, pad_g(qk_q_g), pad_g(qk_k_g), sgbm


def _full(shape):
    return pl.BlockSpec(shape, lambda *_: (0,) * len(shape))


def _layer(x, rope, p):
    B, S, _ = x.shape
    nt = S // TM
    bf16 = jnp.bfloat16
    row = lambda a: a.reshape(1, -1)
    win, wuq, wk, wvt, gq, gk, sgbm = _prep_layer(
        p['w_in'], p['w_uq'], p['w_ukv'], p['qk_q_g'], p['qk_k_g'], p['sg_b'])
    bng = p['branch_norm_g']
    tile = lambda w: pl.BlockSpec((None, TM, w), lambda b, i: (b, i, 0))
    ropespec = pl.BlockSpec((TM, HEAD_PAD), lambda b, i: (i, 0))
    front_in = [
        (x, tile(D_MODEL)),
        (row(p['norm_g']), _full((1, D_MODEL))),
        (win, _full((D_MODEL, IN_COLS_PAD))),
        (p['conv_w'], _full((CONV_K, CONV_WIDTH))),
        (row(p['conv_b']), _full((1, CONV_WIDTH))),
        (row(p['conv_ln_g']), _full((1, CONV_WIDTH))),
        (row(p['conv_ln_b']), _full((1, CONV_WIDTH))),
        (p['conv_pw_w'].astype(bf16), _full((CONV_WIDTH, CONV_WIDTH))),
        (row(p['conv_pw_b']), _full((1, CONV_WIDTH))),
        (row(p['q_norm_g']), _full((1, Q_LORA))),
        (wuq, _full((Q_LORA, QK_PAD))),
        (row(p['kv_norm_g']), _full((1, KV_LORA))),
        (wk, _full((KV_LORA, QK_PAD))),
        (wvt, _full((MLA_WIDTH, KV_LORA))),
        (gq, _full((1, HEAD_PAD))),
        (gk, _full((1, HEAD_PAD))),
        (row(p['sg_ln_g']), _full((1, SG_WIDTH))),
        (row(p['sg_ln_b']), _full((1, SG_WIDTH))),
        (p['sg_w'], _full((SG_HEADS, SG_CHUNK, SG_CHUNK))),
        (sgbm, _full((SG_CHUNK, SG_WIDTH))),
        (row(bng[:CONV_WIDTH]), _full((1, CONV_WIDTH))),
        (row(bng[CONV_WIDTH + MLA_WIDTH:]), _full((1, SG_WIDTH))),
        (rope[0], ropespec), (rope[1], ropespec), (rope[2], ropespec),
    ]
    ycs, q, k, vt, zg = pl.pallas_call(
        _front_kernel,
        grid=(B, nt),
        in_specs=[s for _, s in front_in],
        out_specs=[tile(CONV_WIDTH + SG_WIDTH), tile(QK_PAD), tile(QK_PAD),
                   pl.BlockSpec((None, None, MLA_WIDTH, TM), lambda b, i: (b, i, 0, 0)),
                   tile(MLA_WIDTH)],
        out_shape=[jax.ShapeDtypeStruct((B, S, CONV_WIDTH + SG_WIDTH), bf16),
                   jax.ShapeDtypeStruct((B, S, QK_PAD), bf16),
                   jax.ShapeDtypeStruct((B, S, QK_PAD), bf16),
                   jax.ShapeDtypeStruct((B, nt, MLA_WIDTH, TM), bf16),
                   jax.ShapeDtypeStruct((B, S, MLA_WIDTH), bf16)],
        scratch_shapes=[pltpu.VMEM((CONV_HALO + TM, CONV_WIDTH), jnp.float32)],
        compiler_params=pltpu.CompilerParams(
            dimension_semantics=("arbitrary", "arbitrary"), vmem_limit_bytes=VMEM_LIMIT),
        name="front",
    )(*[a for a, _ in front_in])

    nq, nkv = S // TQ, S // TK
    ot = pl.pallas_call(
        _attn_kernel,
        grid=(B, nq),
        in_specs=[pl.BlockSpec((None, TQ, QK_PAD), lambda b, i: (b, i, 0)),
                  pl.BlockSpec((None, S, QK_PAD), lambda b, i: (b, 0, 0),
                               pipeline_mode=pl.Buffered(1)),
                  pl.BlockSpec((None, nkv, MLA_WIDTH, TK), lambda b, i: (b, 0, 0, 0),
                               pipeline_mode=pl.Buffered(1))],
        out_specs=pl.BlockSpec((None, None, MLA_WIDTH, TQ), lambda b, i: (b, i, 0, 0)),
        out_shape=jax.ShapeDtypeStruct((B, nq, MLA_WIDTH, TQ), jnp.float32),
        scratch_shapes=[pltpu.VMEM((MLA_HEADS, TK, TQ), jnp.float32),
                        pltpu.VMEM((MLA_HEADS, TK, TQ), jnp.float32),
                        pltpu.VMEM((MLA_HEADS, 1, TQ), jnp.float32),
                        pltpu.VMEM((MLA_HEADS, 1, TQ), jnp.float32),
                        pltpu.VMEM((MLA_HEADS, 1, TQ), jnp.float32),
                        pltpu.VMEM((MLA_HEADS, 1, TQ), jnp.float32)],
        compiler_params=pltpu.CompilerParams(
            dimension_semantics=("arbitrary", "arbitrary"), vmem_limit_bytes=VMEM_LIMIT),
        name="attn",
    )(q, k, vt)

    return pl.pallas_call(
        _back_kernel,
        grid=(B, nt),
        in_specs=[tile(D_MODEL), tile(CONV_WIDTH + SG_WIDTH),
                  pl.BlockSpec((None, None, MLA_WIDTH, TM), lambda b, i: (b, i, 0, 0)),
                  tile(MLA_WIDTH), _full((1, MLA_WIDTH)), _full((D_MODEL, D_MODEL))],
        out_specs=tile(D_MODEL),
        out_shape=jax.ShapeDtypeStruct((B, S, D_MODEL), jnp.float32),
        compiler_params=pltpu.CompilerParams(
            dimension_semantics=("arbitrary", "arbitrary"), vmem_limit_bytes=VMEM_LIMIT),
        name="back",
    )(x, ycs, ot, zg, row(bng[CONV_WIDTH:CONV_WIDTH + MLA_WIDTH]), p['w_out'].astype(bf16))


def kernel(x, norm_g, w_in, conv_w, conv_b, conv_ln_g, conv_ln_b, conv_pw_w, conv_pw_b,
           q_norm_g, w_uq, kv_norm_g, w_ukv, qk_q_g, qk_k_g, sg_ln_g, sg_ln_b, sg_w, sg_b,
           branch_norm_g, w_out):
    params = dict(norm_g=norm_g, w_in=w_in, conv_w=conv_w, conv_b=conv_b, conv_ln_g=conv_ln_g,
                  conv_ln_b=conv_ln_b, conv_pw_w=conv_pw_w, conv_pw_b=conv_pw_b,
                  q_norm_g=q_norm_g, w_uq=w_uq, kv_norm_g=kv_norm_g, w_ukv=w_ukv,
                  qk_q_g=qk_q_g, qk_k_g=qk_k_g, sg_ln_g=sg_ln_g, sg_ln_b=sg_ln_b, sg_w=sg_w,
                  sg_b=sg_b, branch_norm_g=branch_norm_g, w_out=w_out)
    rope = _rope_lane_tables(x.shape[1])
    for layer in range(norm_g.shape[0]):
        x = _layer(x, rope, {name: a[layer] for name, a in params.items()})
    return x
```

```python
import functools

import jax
import jax.numpy as jnp
from jax import lax
from jax.experimental import pallas as pl
from jax.experimental.pallas import tpu as pltpu

D_MODEL = 1024
CONV_WIDTH = 256
CONV_K = 31
MLA_HEADS = 8
MLA_NOPE = 64
MLA_ROPE = 32
MLA_QK = MLA_NOPE + MLA_ROPE
MLA_V = 64
MLA_WIDTH = MLA_HEADS * MLA_V
Q_LORA = 768
KV_LORA = 256
ROPE_THETA = 10000.0
SG_WIDTH = 256
SG_HEADS = 4
SG_HEAD_DIM = SG_WIDTH // SG_HEADS
SG_CHUNK = 128
EPS = 1e-6

SUBLANES = 8
LANES = 128
HEAD_PAD = LANES
QK_PAD = MLA_HEADS * HEAD_PAD

OFF_A = 0
OFF_GLU = OFF_A + CONV_WIDTH
OFF_ZC = OFF_GLU + CONV_WIDTH
OFF_CQ = OFF_ZC + CONV_WIDTH
OFF_CKV = OFF_CQ + Q_LORA
OFF_KR = OFF_CKV + KV_LORA
OFF_ZM = OFF_KR + HEAD_PAD
OFF_U = OFF_ZM + MLA_WIDTH
OFF_V = OFF_U + SG_WIDTH
OFF_ZS = OFF_V + SG_WIDTH
IN_COLS_PAD = OFF_ZS + SG_WIDTH

CONV_HALO = 32
TM = 256
FRONT_TILES = 2
TQ = 256
TK = 512
ACC_ROWS = MLA_V + 16
NEG_BIG = -1e30
LOG2E = 1.4426950408889634
VMEM_LIMIT = 48 * 1024 * 1024

assert FRONT_TILES * TM == TK and TM % SG_CHUNK == 0 and TK == 2 * TQ


def _rms(x, g, width):
    ss = jnp.sum(x * x, axis=-1, keepdims=True)
    return x * lax.rsqrt(ss + width * EPS) * (g * width ** 0.5)


def _layer_norm(x, g, b, width):
    mu = jnp.sum(x, axis=-1, keepdims=True) * (1.0 / width)
    xc = x - mu
    ss = jnp.sum(xc * xc, axis=-1, keepdims=True)
    return xc * lax.rsqrt(ss + width * EPS) * (g * width ** 0.5) + b


def _sigmoid(x):
    return 0.5 * jnp.tanh(0.5 * x) + 0.5


def _silu(x):
    return x * _sigmoid(x)


def _rope(x, c, s_up, s_dn):
    return x * c + pltpu.roll(x, LANES - MLA_ROPE // 2, 1) * s_up + pltpu.roll(x, MLA_ROPE // 2, 1) * s_dn


def _front_kernel(h1_ref, h2_ref, h0_ref, win_ref, convw_ref, convb_ref, clng_ref,
                  clnb_ref, pww_ref, pwb_ref, qng_ref, wuq_ref, kvng_ref, wk_ref, wvt_ref,
                  gq_ref, gk_ref, sglng_ref, sglnb_ref, sgw_ref, sgbm_ref,
                  bgc_ref, bgs_ref, rc_ref, rsu_ref, rsd_ref,
                  ycs_ref, q_ref, k_ref, vt_ref, zg_ref,
                  proja_ref, projb_ref, ybuf_ref, zs_ref, *, steps_per_batch):
    f32, bf16 = jnp.float32, jnp.bfloat16
    step = pl.program_id(0)

    def project(h_ref, proj_ref):
        proj_ref[...] = jnp.dot(h_ref[...], win_ref[...], preferred_element_type=f32)

    def mix(proj_ref, half):
        rows = slice(half * TM, (half + 1) * TM)
        col = lambda off, w: proj_ref[:, off:off + w]

        cq = _rms(col(OFF_CQ, Q_LORA), qng_ref[...], Q_LORA).astype(bf16)
        qf = jnp.dot(cq, wuq_ref[...], preferred_element_type=f32)
        ckv = _rms(col(OFF_CKV, KV_LORA), kvng_ref[...], KV_LORA).astype(bf16)
        kf = jnp.dot(ckv, wk_ref[...], preferred_element_type=f32)
        vt = lax.dot_general(wvt_ref[...], ckv, (((1,), (1,)), ((), ())),
                             preferred_element_type=f32)
        vt_ref[:, rows] = vt.astype(bf16)

        v = _layer_norm(jax.nn.gelu(col(OFF_V, SG_WIDTH)), sglng_ref[...], sglnb_ref[...], SG_WIDTH)
        trow = lax.broadcasted_iota(jnp.int32, (SG_CHUNK, SG_CHUNK), 0)
        tcol = lax.broadcasted_iota(jnp.int32, (SG_CHUNK, SG_CHUNK), 1)
        wcat = jnp.concatenate(
            [jnp.where(tcol <= trow, sgw_ref[g], 0.0) for g in range(SG_HEADS)], axis=1).astype(bf16)
        lane = lax.broadcasted_iota(jnp.int32, (SG_CHUNK, SG_WIDTH), 1)
        mixed = []
        for c in range(TM // SG_CHUNK):
            vc = v[c * SG_CHUNK:(c + 1) * SG_CHUNK, :]
            vstack = jnp.concatenate(
                [jnp.where((lane >= g * SG_HEAD_DIM) & (lane < (g + 1) * SG_HEAD_DIM), vc, 0.0)
                 for g in range(SG_HEADS)], axis=0).astype(bf16)
            mixed.append(jnp.dot(wcat, vstack, preferred_element_type=f32) + sgbm_ref[...])
        ys = jax.nn.gelu(col(OFF_U, SG_WIDTH)) * jnp.concatenate(mixed, axis=0) * _silu(col(OFF_ZS, SG_WIDTH))
        ycs_ref[rows, CONV_WIDTH:CONV_WIDTH + SG_WIDTH] = _rms(ys, bgs_ref[...], SG_WIDTH).astype(bf16)

        ybuf_ref[CONV_HALO:CONV_HALO + TM, :] = col(OFF_A, CONV_WIDTH) * _sigmoid(col(OFF_GLU, CONV_WIDTH))
        conv = jnp.zeros((TM, CONV_WIDTH), f32) + convb_ref[...]
        first = CONV_HALO - (CONV_K - 1)
        for r in range(SUBLANES):
            taps = [t for t in range(CONV_K) if (t + first) % SUBLANES == r]
            span = (taps[-1] + first) // SUBLANES * SUBLANES
            if r == 0:
                src = ybuf_ref
            else:
                src = zs_ref.at[r - 1]
                src[0:span + TM, :] = ybuf_ref[r:r + span + TM, :]
            for t in taps:
                off = (t + first) // SUBLANES * SUBLANES
                conv = conv + src[off:off + TM, :] * convw_ref[t:t + 1, :]
        ybuf_ref[0:CONV_HALO, :] = ybuf_ref[TM:TM + CONV_HALO, :]
        yc = _silu(_layer_norm(conv, clng_ref[...], clnb_ref[...], CONV_WIDTH)).astype(bf16)
        yc = jnp.dot(yc, pww_ref[...], preferred_element_type=f32) + pwb_ref[...]
        yc = yc * _silu(col(OFF_ZC, CONV_WIDTH))
        ycs_ref[rows, 0:CONV_WIDTH] = _rms(yc, bgc_ref[...], CONV_WIDTH).astype(bf16)

        zg_ref[rows, :] = _silu(col(OFF_ZM, MLA_WIDTH)).astype(bf16)
        rc, rsu, rsd = rc_ref[rows, :], rsu_ref[rows, :], rsd_ref[rows, :]
        qc, qsu, qsd = (rc * (gq_ref[0:1, :] * LOG2E), rsu * (gq_ref[1:2, :] * LOG2E),
                        rsd * (gq_ref[2:3, :] * LOG2E))
        for hd in range(MLA_HEADS):
            qh = qf[:, hd * HEAD_PAD:(hd + 1) * HEAD_PAD]
            ss = jnp.sum(qh * qh, axis=-1, keepdims=True)
            q_ref[rows, hd * HEAD_PAD:(hd + 1) * HEAD_PAD] = (
                _rope(qh, qc, qsu, qsd) * lax.rsqrt(ss + MLA_QK * EPS)).astype(bf16)
        gk = gk_ref[...] * MLA_QK ** 0.5
        kr = col(OFF_KR, HEAD_PAD)
        ss_r = jnp.sum(kr * kr, axis=-1, keepdims=True)
        krr = _rope(kr * gk, rc, rsu, rsd)
        for hd in range(MLA_HEADS):
            kh = kf[:, hd * HEAD_PAD:(hd + 1) * HEAD_PAD]
            ss = jnp.sum(kh * kh, axis=-1, keepdims=True) + ss_r
            k_ref[rows, hd * HEAD_PAD:(hd + 1) * HEAD_PAD] = (
                (kh * gk + krr) * lax.rsqrt(ss + MLA_QK * EPS)).astype(bf16)

    @pl.when(step == 0)
    def _():
        project(h0_ref, proja_ref)

    @pl.when(step % steps_per_batch == 0)
    def _():
        ybuf_ref[0:CONV_HALO, :] = jnp.zeros((CONV_HALO, CONV_WIDTH), f32)

    project(h1_ref, projb_ref)
    mix(proja_ref, 0)
    project(h2_ref, proja_ref)
    mix(projb_ref, 1)


def _col_max8(s):
    m8 = jnp.max(s.reshape(s.shape[0] // SUBLANES, SUBLANES, s.shape[1]), axis=0)
    for shift in (4, 2, 1):
        m8 = jnp.maximum(m8, pltpu.roll(m8, shift, 0))
    return m8


def _rows8(x, fn, stat8):
    x3 = x.reshape(x.shape[0] // SUBLANES, SUBLANES, x.shape[1])
    return fn(x3, stat8[None]).reshape(x.shape)


def _attn_kernel(q_ref, k_ref, vt_ref, o_ref, sa_ref, sb_ref, maxa_ref, maxb_ref, m_ref, acc_ref):
    f32, bf16 = jnp.float32, jnp.bfloat16
    qi = pl.program_id(1)
    n_full = qi // (TK // TQ)
    diag_shift = (qi % (TK // TQ)) * TQ
    key_minus_query = (lax.broadcasted_iota(jnp.int32, (TK, TQ), 0)
                       - lax.broadcasted_iota(jnp.int32, (TK, TQ), 1))
    ones_rows = jnp.ones((ACC_ROWS - MLA_V, TK), bf16)
    heads = range(MLA_HEADS)
    hslab = lambda hh: slice(hh * HEAD_PAD, (hh + 1) * HEAD_PAD)
    vslab = lambda hh: slice(hh * MLA_V, (hh + 1) * MLA_V)

    def scores(j, s_ref, max_ref, hh):
        kb = k_ref[pl.ds(pl.multiple_of(j * TK, TK), TK), hslab(hh)]
        s = lax.dot_general(kb, q_ref[:, hslab(hh)], (((1,), (1,)), ((), ())),
                            preferred_element_type=f32)
        s_ref[hh] = s
        max_ref[hh] = _col_max8(s)

    def consume(j, s_ref, max_ref, hh, masked):
        s = s_ref[hh]
        if masked:
            s = jnp.where(key_minus_query <= diag_shift, s, NEG_BIG)
            blk_max = _col_max8(s)
        else:
            blk_max = max_ref[hh]
        m = m_ref[hh]
        m_new = jnp.maximum(m, blk_max)
        alpha = jnp.exp2(m - m_new)
        p = _rows8(s, lambda x, st: jnp.exp2(x - st), m_new)
        m_ref[hh] = m_new
        lhs = jnp.concatenate([vt_ref[j, vslab(hh), :], ones_rows], axis=0)
        pv = jnp.dot(lhs, p.astype(bf16), preferred_element_type=f32)
        acc_ref[hh] = _rows8(acc_ref[hh], lambda x, st: x * st, alpha) + pv

    def half_step(j, cur, nxt):
        for hh in heads:
            scores(j + 1, nxt[0], nxt[1], hh)
            consume(j, cur[0], cur[1], hh, False)

    buf_a, buf_b = (sa_ref, maxa_ref), (sb_ref, maxb_ref)
    m_ref[...] = jnp.full(m_ref.shape, NEG_BIG, f32)
    acc_ref[...] = jnp.zeros(acc_ref.shape, f32)
    for hh in heads:
        scores(0, sa_ref, maxa_ref, hh)

    def pair(jj, carry):
        half_step(2 * jj, buf_a, buf_b)
        half_step(2 * jj + 1, buf_b, buf_a)
        return carry

    lax.fori_loop(0, n_full // 2, pair, 0)

    @pl.when(n_full % 2 == 1)
    def _():
        half_step(n_full - 1, buf_a, buf_b)
        for hh in heads:
            consume(n_full, sb_ref, maxb_ref, hh, True)

    @pl.when(n_full % 2 == 0)
    def _():
        for hh in heads:
            consume(n_full, sa_ref, maxa_ref, hh, True)

    for hh in heads:
        acc = acc_ref[hh]
        o_ref[vslab(hh), :] = _rows8(acc[:MLA_V], lambda x, st: x / st, acc[MLA_V:MLA_V + SUBLANES])


def _norm_kernel(x_ref, g_ref, h_ref):
    h_ref[...] = _rms(x_ref[...], g_ref[...], D_MODEL).astype(jnp.bfloat16)


def _back_kernel(x_ref, ycs_ref, ot_ref, zg_ref, bgm_ref, wout_ref, *rest):
    f32, bf16 = jnp.float32, jnp.bfloat16
    ym = ot_ref[...].T * zg_ref[...].astype(f32)
    ym = _rms(ym, bgm_ref[...], MLA_WIDTH).astype(bf16)
    ycs = ycs_ref[...]
    y = jnp.concatenate([ycs[:, 0:CONV_WIDTH], ym, ycs[:, CONV_WIDTH:]], axis=1)
    x_new = x_ref[...] + jnp.dot(y, wout_ref[...], preferred_element_type=f32)
    if len(rest) == 1:
        rest[0][...] = x_new
    else:
        next_g_ref, out_ref, next_h_ref = rest
        out_ref[...] = x_new
        next_h_ref[...] = _rms(x_new, next_g_ref[...], D_MODEL).astype(bf16)


def _rope_lane_tables(seq):
    half = MLA_ROPE // 2
    inv_freq = ROPE_THETA ** (-jnp.arange(half, dtype=jnp.float32) / half)
    ang = jnp.arange(seq, dtype=jnp.float32)[:, None] * inv_freq[None, :]
    cos, sin = jnp.cos(ang), jnp.sin(ang)
    ones = jnp.ones((seq, MLA_NOPE), jnp.float32)
    zpad = jnp.zeros((seq, HEAD_PAD - MLA_QK), jnp.float32)
    zn = jnp.zeros((seq, MLA_NOPE), jnp.float32)
    zh = jnp.zeros((seq, half), jnp.float32)
    c = jnp.concatenate([ones, cos, cos, zpad], axis=1)
    s_up = jnp.concatenate([zn, -sin, zh, zpad], axis=1)
    s_dn = jnp.concatenate([zn, zh, sin, zpad], axis=1)
    return c, s_up, s_dn


def _prep_layer(w_in, w_uq, w_ukv, qk_q_g, qk_k_g, sg_b):
    bf16 = jnp.bfloat16
    zc = lambda n: jnp.zeros((w_in.shape[0], n), w_in.dtype)
    kr0 = 3 * CONV_WIDTH + Q_LORA + KV_LORA
    win = jnp.concatenate(
        [w_in[:, :kr0], zc(MLA_NOPE), w_in[:, kr0:kr0 + MLA_ROPE], zc(HEAD_PAD - MLA_QK),
         w_in[:, kr0 + MLA_ROPE:]], axis=1).astype(bf16)
    wuq = jnp.pad(w_uq.reshape(Q_LORA, MLA_HEADS, MLA_QK),
                  ((0, 0), (0, 0), (0, HEAD_PAD - MLA_QK))).reshape(Q_LORA, QK_PAD).astype(bf16)
    wukv = w_ukv.reshape(KV_LORA, MLA_HEADS, MLA_NOPE + MLA_V)
    wk = jnp.pad(wukv[:, :, :MLA_NOPE],
                 ((0, 0), (0, 0), (0, HEAD_PAD - MLA_NOPE))).reshape(KV_LORA, QK_PAD).astype(bf16)
    wvt = wukv[:, :, MLA_NOPE:].reshape(KV_LORA, MLA_WIDTH).T.astype(bf16)
    pad_g = lambda g: jnp.pad(g, (0, HEAD_PAD - MLA_QK)).reshape(1, HEAD_PAD)
    gq = pad_g(qk_q_g)
    gq3 = jnp.concatenate([gq, jnp.roll(gq, -(MLA_ROPE // 2), 1), jnp.roll(gq, MLA_ROPE // 2, 1)], axis=0)
    sgbm = jnp.repeat(sg_b.T, SG_HEAD_DIM, axis=1)
    return win, wuq, wk, wvt, gq3, pad_g(qk_k_g), sgbm


def _const(shape):
    return pl.BlockSpec(shape, lambda *_: (0,) * len(shape), pipeline_mode=pl.Buffered(1))


def _normalise(x, g):
    n_tok = x.shape[0] * x.shape[1]
    tile = pl.BlockSpec((TK, D_MODEL), lambda i: (i, 0))
    return pl.pallas_call(
        _norm_kernel, grid=(n_tok // TK,),
        in_specs=[tile, _const((1, D_MODEL))], out_specs=tile,
        out_shape=jax.ShapeDtypeStruct((n_tok, D_MODEL), jnp.bfloat16),
        compiler_params=pltpu.CompilerParams(dimension_semantics=("arbitrary",),
                                             vmem_limit_bytes=VMEM_LIMIT),
        name="norm",
    )(x.reshape(n_tok, D_MODEL), g.reshape(1, D_MODEL))


def _layer(x, h, rope, p, next_norm_g):
    B, S, _ = x.shape
    n_tok = B * S
    n_tiles = n_tok // TM
    tstep = FRONT_TILES * TM
    n_steps = n_tok // tstep
    steps_per_batch = S // tstep
    bf16 = jnp.bfloat16
    row = lambda a: a.reshape(1, -1)
    win, wuq, wk, wvt, gq, gk, sgbm = _prep_layer(
        p['w_in'], p['w_uq'], p['w_ukv'], p['qk_q_g'], p['qk_k_g'], p['sg_b'])
    bng = p['branch_norm_g']
    htile = lambda fn: pl.BlockSpec((TM, D_MODEL), fn)
    ropespec = pl.BlockSpec((tstep, HEAD_PAD), lambda s: (s % steps_per_batch, 0))
    front_in = [
        (h, htile(lambda s: (FRONT_TILES * s + 1, 0))),
        (h, htile(lambda s: (jnp.minimum(FRONT_TILES * s + 2, n_tiles - 1), 0))),
        (h, htile(lambda s: (0, 0))),
        (win, _const((D_MODEL, IN_COLS_PAD))),
        (p['conv_w'], _const((CONV_K, CONV_WIDTH))),
        (row(p['conv_b']), _const((1, CONV_WIDTH))),
        (row(p['conv_ln_g']), _const((1, CONV_WIDTH))),
        (row(p['conv_ln_b']), _const((1, CONV_WIDTH))),
        (p['conv_pw_w'].astype(bf16), _const((CONV_WIDTH, CONV_WIDTH))),
        (row(p['conv_pw_b']), _const((1, CONV_WIDTH))),
        (row(p['q_norm_g']), _const((1, Q_LORA))),
        (wuq, _const((Q_LORA, QK_PAD))),
        (row(p['kv_norm_g']), _const((1, KV_LORA))),
        (wk, _const((KV_LORA, QK_PAD))),
        (wvt, _const((MLA_WIDTH, KV_LORA))),
        (gq, _const((3, HEAD_PAD))),
        (gk, _const((1, HEAD_PAD))),
        (row(p['sg_ln_g']), _const((1, SG_WIDTH))),
        (row(p['sg_ln_b']), _const((1, SG_WIDTH))),
        (p['sg_w'], _const((SG_HEADS, SG_CHUNK, SG_CHUNK))),
        (sgbm, _const((SG_CHUNK, SG_WIDTH))),
        (row(bng[:CONV_WIDTH]), _const((1, CONV_WIDTH))),
        (row(bng[CONV_WIDTH + MLA_WIDTH:]), _const((1, SG_WIDTH))),
        (rope[0], ropespec), (rope[1], ropespec), (rope[2], ropespec),
    ]
    otile = lambda w: pl.BlockSpec((tstep, w), lambda s: (s, 0))
    ycs, q, k, vt, zg = pl.pallas_call(
        functools.partial(_front_kernel, steps_per_batch=steps_per_batch),
        grid=(n_steps,),
        in_specs=[s for _, s in front_in],
        out_specs=[otile(CONV_WIDTH + SG_WIDTH), otile(QK_PAD), otile(QK_PAD),
                   pl.BlockSpec((None, MLA_WIDTH, TK), lambda s: (s, 0, 0)),
                   otile(MLA_WIDTH)],
        out_shape=[jax.ShapeDtypeStruct((n_tok, CONV_WIDTH + SG_WIDTH), bf16),
                   jax.ShapeDtypeStruct((n_tok, QK_PAD), bf16),
                   jax.ShapeDtypeStruct((n_tok, QK_PAD), bf16),
                   jax.ShapeDtypeStruct((n_tok // TK, MLA_WIDTH, TK), bf16),
                   jax.ShapeDtypeStruct((n_tok, MLA_WIDTH), bf16)],
        scratch_shapes=[pltpu.VMEM((TM, IN_COLS_PAD), jnp.float32),
                        pltpu.VMEM((TM, IN_COLS_PAD), jnp.float32),
                        pltpu.VMEM((CONV_HALO + TM, CONV_WIDTH), jnp.float32),
                        pltpu.VMEM((SUBLANES - 1, CONV_HALO + TM, CONV_WIDTH), jnp.float32)],
        compiler_params=pltpu.CompilerParams(
            dimension_semantics=("arbitrary",), vmem_limit_bytes=VMEM_LIMIT),
        name="front",
    )(*[a for a, _ in front_in])

    nq, nkv = S // TQ, S // TK
    ot = pl.pallas_call(
        _attn_kernel,
        grid=(B, nq),
        in_specs=[pl.BlockSpec((None, TQ, QK_PAD), lambda b, i: (b, i, 0)),
                  pl.BlockSpec((None, S, QK_PAD), lambda b, i: (b, 0, 0),
                               pipeline_mode=pl.Buffered(1)),
                  pl.BlockSpec((None, nkv, MLA_WIDTH, TK), lambda b, i: (b, 0, 0, 0),
                               pipeline_mode=pl.Buffered(1))],
        out_specs=pl.BlockSpec((None, None, MLA_WIDTH, TQ), lambda b, i: (b, i, 0, 0)),
        out_shape=jax.ShapeDtypeStruct((B, nq, MLA_WIDTH, TQ), jnp.float32),
        scratch_shapes=[pltpu.VMEM((MLA_HEADS, TK, TQ), jnp.float32),
                        pltpu.VMEM((MLA_HEADS, TK, TQ), jnp.float32),
                        pltpu.VMEM((MLA_HEADS, SUBLANES, TQ), jnp.float32),
                        pltpu.VMEM((MLA_HEADS, SUBLANES, TQ), jnp.float32),
                        pltpu.VMEM((MLA_HEADS, SUBLANES, TQ), jnp.float32),
                        pltpu.VMEM((MLA_HEADS, ACC_ROWS, TQ), jnp.float32)],
        compiler_params=pltpu.CompilerParams(
            dimension_semantics=("arbitrary", "arbitrary"), vmem_limit_bytes=VMEM_LIMIT),
        name="attn",
    )(q.reshape(B, S, QK_PAD), k.reshape(B, S, QK_PAD), vt.reshape(B, nkv, MLA_WIDTH, TK))

    tile = lambda w: pl.BlockSpec((None, TM, w), lambda b, i: (b, i, 0))
    back_in = [(x, tile(D_MODEL)), (ycs.reshape(B, S, -1), tile(CONV_WIDTH + SG_WIDTH)),
               (ot, pl.BlockSpec((None, None, MLA_WIDTH, TM), lambda b, i: (b, i, 0, 0))),
               (zg.reshape(B, S, -1), tile(MLA_WIDTH)),
               (row(bng[CONV_WIDTH:CONV_WIDTH + MLA_WIDTH]), _const((1, MLA_WIDTH))),
               (p['w_out'].astype(bf16), _const((D_MODEL, D_MODEL)))]
    out_specs = [tile(D_MODEL)]
    out_shape = [jax.ShapeDtypeStruct((B, S, D_MODEL), jnp.float32)]
    if next_norm_g is not None:
        back_in.append((row(next_norm_g), _const((1, D_MODEL))))
        out_specs.append(tile(D_MODEL))
        out_shape.append(jax.ShapeDtypeStruct((B, S, D_MODEL), bf16))
    outs = pl.pallas_call(
        _back_kernel,
        grid=(B, S // TM),
        in_specs=[s for _, s in back_in],
        out_specs=out_specs,
        out_shape=out_shape,
        compiler_params=pltpu.CompilerParams(
            dimension_semantics=("arbitrary", "arbitrary"), vmem_limit_bytes=VMEM_LIMIT),
        name="back",
    )(*[a for a, _ in back_in])
    if next_norm_g is None:
        return outs[0], None
    return outs[0], outs[1].reshape(n_tok, D_MODEL)


def kernel(x, norm_g, w_in, conv_w, conv_b, conv_ln_g, conv_ln_b, conv_pw_w, conv_pw_b,
           q_norm_g, w_uq, kv_norm_g, w_ukv, qk_q_g, qk_k_g, sg_ln_g, sg_ln_b, sg_w, sg_b,
           branch_norm_g, w_out):
    params = dict(norm_g=norm_g, w_in=w_in, conv_w=conv_w, conv_b=conv_b, conv_ln_g=conv_ln_g,
                  conv_ln_b=conv_ln_b, conv_pw_w=conv_pw_w, conv_pw_b=conv_pw_b,
                  q_norm_g=q_norm_g, w_uq=w_uq, kv_norm_g=kv_norm_g, w_ukv=w_ukv,
                  qk_q_g=qk_q_g, qk_k_g=qk_k_g, sg_ln_g=sg_ln_g, sg_ln_b=sg_ln_b, sg_w=sg_w,
                  sg_b=sg_b, branch_norm_g=branch_norm_g, w_out=w_out)
    rope = _rope_lane_tables(x.shape[1])
    depth = norm_g.shape[0]
    h = _normalise(x, norm_g[0])
    for layer in range(depth):
        next_g = norm_g[layer + 1] if layer + 1 < depth else None
        x, h = _layer(x, h, rope, {name: a[layer] for name, a in params.items()}, next_g)
    return x
```

```python
import functools

import jax
import jax.numpy as jnp
from jax import lax
from jax.experimental import pallas as pl
from jax.experimental.pallas import tpu as pltpu

D_MODEL = 1024
CONV_WIDTH = 256
CONV_K = 31
MLA_HEADS = 8
MLA_NOPE = 64
MLA_ROPE = 32
MLA_QK = MLA_NOPE + MLA_ROPE
MLA_V = 64
MLA_WIDTH = MLA_HEADS * MLA_V
Q_LORA = 768
KV_LORA = 256
ROPE_THETA = 10000.0
SG_WIDTH = 256
SG_HEADS = 4
SG_HEAD_DIM = SG_WIDTH // SG_HEADS
SG_CHUNK = 128
EPS = 1e-6

SUBLANES = 8
LANES = 128
HEAD_PAD = LANES
QK_PAD = MLA_HEADS * HEAD_PAD

OFF_A = 0
OFF_GLU = OFF_A + CONV_WIDTH
OFF_ZC = OFF_GLU + CONV_WIDTH
OFF_CQ = OFF_ZC + CONV_WIDTH
OFF_CKV = OFF_CQ + Q_LORA
OFF_KR = OFF_CKV + KV_LORA
OFF_ZM = OFF_KR + HEAD_PAD
OFF_U = OFF_ZM + MLA_WIDTH
OFF_V = OFF_U + SG_WIDTH
OFF_ZS = OFF_V + SG_WIDTH
IN_COLS_PAD = OFF_ZS + SG_WIDTH

CONV_HALO = 32
CONV_ROWS = 64
TM = 256
TB = 512
FRONT_TILES = 4
TQ = 256
TK = 512
ACC_ROWS = MLA_V + 16
NEG_BIG = -1e30
LOG2E = 1.4426950408889634
VMEM_LIMIT = 48 * 1024 * 1024

assert FRONT_TILES >= 2 and (FRONT_TILES * TM) % TK == 0 and TK % TM == 0 and TM % SG_CHUNK == 0 and TK == 2 * TQ


def _rms(x, g, width):
    ss = jnp.sum(x * x, axis=-1, keepdims=True)
    return x * lax.rsqrt(ss + width * EPS) * (g * width ** 0.5)


def _layer_norm(x, g, b, width):
    mu = jnp.sum(x, axis=-1, keepdims=True) * (1.0 / width)
    xc = x - mu
    ss = jnp.sum(xc * xc, axis=-1, keepdims=True)
    return xc * lax.rsqrt(ss + width * EPS) * (g * width ** 0.5) + b


def _sigmoid(x):
    return 0.5 * jnp.tanh(0.5 * x) + 0.5


def _silu(x):
    return x * _sigmoid(x)


def _rope(x, c, s_up, s_dn):
    return x * c + pltpu.roll(x, LANES - MLA_ROPE // 2, 1) * s_up + pltpu.roll(x, MLA_ROPE // 2, 1) * s_dn


def _front_kernel(h_ref, hnext_ref, win_ref, convw_ref, convb_ref, clng_ref,
                  clnb_ref, pww_ref, pwb_ref, qng_ref, wuq_ref, kvng_ref, wk_ref, wvt_ref,
                  gq_ref, gk_ref, sglng_ref, sglnb_ref, sgw_ref, sgbm_ref,
                  bgc_ref, bgs_ref, rc_ref, rsu_ref, rsd_ref,
                  ycs_ref, q_ref, k_ref, vt_ref, zg_ref,
                  proj0_ref, ybuf_ref, zs_ref, *, steps_per_batch):
    f32, bf16 = jnp.float32, jnp.bfloat16
    step = pl.program_id(0)

    def project(h_tile):
        return jnp.dot(h_tile, win_ref[...], preferred_element_type=f32)

    tile_rows = lambda t: h_ref[t * TM:(t + 1) * TM, :]

    def up_project(proj, t):
        cq = _rms(proj[:, OFF_CQ:OFF_CQ + Q_LORA], qng_ref[...], Q_LORA).astype(bf16)
        qf = jnp.dot(cq, wuq_ref[...], preferred_element_type=f32)
        ckv = _rms(proj[:, OFF_CKV:OFF_CKV + KV_LORA], kvng_ref[...], KV_LORA).astype(bf16)
        kf = jnp.dot(ckv, wk_ref[...], preferred_element_type=f32)
        vt = lax.dot_general(wvt_ref[...], ckv, (((1,), (1,)), ((), ())),
                             preferred_element_type=f32)
        vt_ref[t * TM // TK, :, (t * TM) % TK:(t * TM) % TK + TM] = vt.astype(bf16)
        return qf, kf

    def mix(proj, qf, kf, t):
        rows = slice(t * TM, (t + 1) * TM)
        col = lambda off, w: proj[:, off:off + w]

        v = _layer_norm(jax.nn.gelu(col(OFF_V, SG_WIDTH)), sglng_ref[...], sglnb_ref[...], SG_WIDTH)
        trow = lax.broadcasted_iota(jnp.int32, (SG_CHUNK, SG_CHUNK), 0)
        tcol = lax.broadcasted_iota(jnp.int32, (SG_CHUNK, SG_CHUNK), 1)
        wcat = jnp.concatenate(
            [jnp.where(tcol <= trow, sgw_ref[g], 0.0) for g in range(SG_HEADS)], axis=1).astype(bf16)
        lane = lax.broadcasted_iota(jnp.int32, (SG_CHUNK, SG_WIDTH), 1)
        mixed = []
        for c in range(TM // SG_CHUNK):
            vc = v[c * SG_CHUNK:(c + 1) * SG_CHUNK, :]
            vstack = jnp.concatenate(
                [jnp.where((lane >= g * SG_HEAD_DIM) & (lane < (g + 1) * SG_HEAD_DIM), vc, 0.0)
                 for g in range(SG_HEADS)], axis=0).astype(bf16)
            mixed.append(jnp.dot(wcat, vstack, preferred_element_type=f32) + sgbm_ref[...])
        ys = jax.nn.gelu(col(OFF_U, SG_WIDTH)) * jnp.concatenate(mixed, axis=0) * _silu(col(OFF_ZS, SG_WIDTH))
        ycs_ref[rows, CONV_WIDTH:CONV_WIDTH + SG_WIDTH] = _rms(ys, bgs_ref[...], SG_WIDTH).astype(bf16)

        ybuf_ref[CONV_HALO:CONV_HALO + TM, :] = col(OFF_A, CONV_WIDTH) * _sigmoid(col(OFF_GLU, CONV_WIDTH))
        first = CONV_HALO - (CONV_K - 1)
        for r in range(1, SUBLANES):
            zs_ref[r - 1, 0:TM + CONV_HALO - SUBLANES, :] = ybuf_ref[r:r + TM + CONV_HALO - SUBLANES, :]
        chunks = []
        for c0 in range(0, TM, CONV_ROWS):
            conv = jnp.zeros((CONV_ROWS, CONV_WIDTH), f32) + convb_ref[...]
            for tap in range(CONV_K):
                r, off = (tap + first) % SUBLANES, (tap + first) // SUBLANES * SUBLANES + c0
                src = ybuf_ref if r == 0 else zs_ref.at[r - 1]
                conv = conv + src[off:off + CONV_ROWS, :] * convw_ref[tap:tap + 1, :]
            chunks.append(_silu(_layer_norm(conv, clng_ref[...], clnb_ref[...], CONV_WIDTH)).astype(bf16))
        ybuf_ref[0:CONV_HALO, :] = ybuf_ref[TM:TM + CONV_HALO, :]
        yc = jnp.concatenate(chunks, axis=0)
        yc = jnp.dot(yc, pww_ref[...], preferred_element_type=f32) + pwb_ref[...]
        yc = yc * _silu(col(OFF_ZC, CONV_WIDTH))
        ycs_ref[rows, 0:CONV_WIDTH] = _rms(yc, bgc_ref[...], CONV_WIDTH).astype(bf16)

        zg_ref[rows, :] = _silu(col(OFF_ZM, MLA_WIDTH)).astype(bf16)
        rc, rsu, rsd = rc_ref[rows, :], rsu_ref[rows, :], rsd_ref[rows, :]
        qc = rc * (gq_ref[0:1, :] * LOG2E)
        qs = (rsu + rsd) * (gq_ref[1:2, :] * LOG2E)
        real = (lax.broadcasted_iota(jnp.int32, (1, HEAD_PAD), 1) < MLA_QK).astype(f32)
        for hd in range(MLA_HEADS):
            qh = qf[:, hd * HEAD_PAD:(hd + 1) * HEAD_PAD]
            ss = jnp.sum(qh * qh * real, axis=-1, keepdims=True)
            roped = qh * qc + pltpu.roll(qh, LANES - MLA_ROPE, 1) * qs
            q_ref[rows, hd * HEAD_PAD:(hd + 1) * HEAD_PAD] = (
                roped * lax.rsqrt(ss + MLA_QK * EPS)).astype(bf16)
        gk = gk_ref[...] * MLA_QK ** 0.5
        kr = col(OFF_KR, HEAD_PAD)
        ss_r = jnp.sum(kr * kr, axis=-1, keepdims=True)
        krr = _rope(kr * gk, rc, rsu, rsd)
        for hd in range(MLA_HEADS):
            kh = kf[:, hd * HEAD_PAD:(hd + 1) * HEAD_PAD]
            ss = jnp.sum(kh * kh, axis=-1, keepdims=True) + ss_r
            k_ref[rows, hd * HEAD_PAD:(hd + 1) * HEAD_PAD] = (
                (kh * gk + krr) * lax.rsqrt(ss + MLA_QK * EPS)).astype(bf16)

    @pl.when(step % steps_per_batch == 0)
    def _():
        ybuf_ref[0:CONV_HALO, :] = jnp.zeros((CONV_HALO, CONV_WIDTH), f32)

    @pl.when(step == 0)
    def _():
        proj0_ref[...] = project(tile_rows(0))

    projs = {0: proj0_ref, 1: project(tile_rows(1))}
    for t in range(FRONT_TILES):
        proj = projs.pop(t)
        qf, kf = up_project(proj, t)
        if t + 2 < FRONT_TILES:
            projs[t + 2] = project(tile_rows(t + 2))
        mix(proj, qf, kf, t)
    proj0_ref[...] = project(hnext_ref[...])


def _col_max8(s):
    m8 = jnp.max(s.reshape(s.shape[0] // SUBLANES, SUBLANES, s.shape[1]), axis=0)
    for shift in (4, 2, 1):
        m8 = jnp.maximum(m8, pltpu.roll(m8, shift, 0))
    return m8


def _rows8(x, fn, stat8):
    x3 = x.reshape(x.shape[0] // SUBLANES, SUBLANES, x.shape[1])
    return fn(x3, stat8[None]).reshape(x.shape)


def _attn_kernel(q_ref, k_ref, vt_ref, o_ref, sa_ref, sb_ref, maxa_ref, maxb_ref, m_ref, acc_ref):
    f32, bf16 = jnp.float32, jnp.bfloat16
    qi = pl.program_id(1)
    n_full = qi // (TK // TQ)
    diag_shift = (qi % (TK // TQ)) * TQ
    key_minus_query = (lax.broadcasted_iota(jnp.int32, (TK, TQ), 0)
                       - lax.broadcasted_iota(jnp.int32, (TK, TQ), 1))
    ones_rows = jnp.ones((ACC_ROWS - MLA_V, TK), bf16)
    heads = range(MLA_HEADS)
    hslab = lambda hh: slice(hh * HEAD_PAD, (hh + 1) * HEAD_PAD)
    vslab = lambda hh: slice(hh * MLA_V, (hh + 1) * MLA_V)

    def scores(j, s_ref, max_ref, hh):
        kb = k_ref[pl.ds(pl.multiple_of(j * TK, TK), TK), hslab(hh)]
        s = lax.dot_general(kb, q_ref[:, hslab(hh)], (((1,), (1,)), ((), ())),
                            preferred_element_type=f32)
        s_ref[hh] = s
        max_ref[hh] = _col_max8(s)

    def consume(j, s_ref, max_ref, hh, masked):
        s = s_ref[hh]
        if masked:
            s = jnp.where(key_minus_query <= diag_shift, s, NEG_BIG)
            blk_max = _col_max8(s)
        else:
            blk_max = max_ref[hh]
        m = m_ref[hh]
        m_new = jnp.maximum(m, blk_max)
        alpha = jnp.exp2(m - m_new)
        p = _rows8(s, lambda x, st: jnp.exp2(x - st), m_new)
        m_ref[hh] = m_new
        lhs = jnp.concatenate([vt_ref[j, vslab(hh), :], ones_rows], axis=0)
        pv = jnp.dot(lhs, p.astype(bf16), preferred_element_type=f32)
        acc_ref[hh] = _rows8(acc_ref[hh], lambda x, st: x * st, alpha) + pv

    def half_step(j, cur, nxt):
        for hh in heads:
            scores(j + 1, nxt[0], nxt[1], hh)
            consume(j, cur[0], cur[1], hh, False)

    buf_a, buf_b = (sa_ref, maxa_ref), (sb_ref, maxb_ref)
    m_ref[...] = jnp.full(m_ref.shape, NEG_BIG, f32)
    acc_ref[...] = jnp.zeros(acc_ref.shape, f32)
    for hh in heads:
        scores(0, sa_ref, maxa_ref, hh)

    def pair(jj, carry):
        half_step(2 * jj, buf_a, buf_b)
        half_step(2 * jj + 1, buf_b, buf_a)
        return carry

    lax.fori_loop(0, n_full // 2, pair, 0)

    @pl.when(n_full % 2 == 1)
    def _():
        half_step(n_full - 1, buf_a, buf_b)
        for hh in heads:
            consume(n_full, sb_ref, maxb_ref, hh, True)

    @pl.when(n_full % 2 == 0)
    def _():
        for hh in heads:
            consume(n_full, sa_ref, maxa_ref, hh, True)

    out_t = jnp.concatenate(
        [_rows8(acc_ref[hh][:MLA_V], lambda x, st: x / st, acc_ref[hh][MLA_V:MLA_V + SUBLANES])
         for hh in heads], axis=0)
    o_ref[...] = out_t.T.astype(bf16)


def _norm_kernel(x_ref, g_ref, h_ref):
    h_ref[...] = _rms(x_ref[...], g_ref[...], D_MODEL).astype(jnp.bfloat16)


def _back_kernel(x_ref, ycs_ref, ot_ref, zg_ref, bgm_ref, wout_ref, *rest):
    f32, bf16 = jnp.float32, jnp.bfloat16
    ym = ot_ref[...].astype(f32) * zg_ref[...].astype(f32)
    ym = _rms(ym, bgm_ref[...], MLA_WIDTH).astype(bf16)
    ycs = ycs_ref[...]
    y = jnp.concatenate([ycs[:, 0:CONV_WIDTH], ym, ycs[:, CONV_WIDTH:]], axis=1)
    x_new = x_ref[...] + jnp.dot(y, wout_ref[...], preferred_element_type=f32)
    if len(rest) == 1:
        rest[0][...] = x_new
    else:
        next_g_ref, out_ref, next_h_ref = rest
        out_ref[...] = x_new
        next_h_ref[...] = _rms(x_new, next_g_ref[...], D_MODEL).astype(bf16)


def _rope_lane_tables(seq):
    half = MLA_ROPE // 2
    inv_freq = ROPE_THETA ** (-jnp.arange(half, dtype=jnp.float32) / half)
    ang = jnp.arange(seq, dtype=jnp.float32)[:, None] * inv_freq[None, :]
    cos, sin = jnp.cos(ang), jnp.sin(ang)
    ones = jnp.ones((seq, MLA_NOPE), jnp.float32)
    zpad = jnp.zeros((seq, HEAD_PAD - MLA_QK), jnp.float32)
    zn = jnp.zeros((seq, MLA_NOPE), jnp.float32)
    zh = jnp.zeros((seq, half), jnp.float32)
    c = jnp.concatenate([ones, cos, cos, zpad], axis=1)
    s_up = jnp.concatenate([zn, -sin, zh, zpad], axis=1)
    s_dn = jnp.concatenate([zn, zh, sin, zpad], axis=1)
    return c, s_up, s_dn


def _prep_layer(w_in, w_uq, w_ukv, qk_q_g, qk_k_g, sg_b):
    bf16 = jnp.bfloat16
    zc = lambda n: jnp.zeros((w_in.shape[0], n), w_in.dtype)
    kr0 = 3 * CONV_WIDTH + Q_LORA + KV_LORA
    win = jnp.concatenate(
        [w_in[:, :kr0], zc(MLA_NOPE), w_in[:, kr0:kr0 + MLA_ROPE], zc(HEAD_PAD - MLA_QK),
         w_in[:, kr0 + MLA_ROPE:]], axis=1).astype(bf16)
    half = MLA_ROPE // 2
    wq = w_uq.reshape(Q_LORA, MLA_HEADS, MLA_QK)
    wuq = jnp.concatenate([wq, wq[:, :, MLA_NOPE + half:], wq[:, :, MLA_NOPE:MLA_NOPE + half]],
                          axis=2).reshape(Q_LORA, QK_PAD).astype(bf16)
    wukv = w_ukv.reshape(KV_LORA, MLA_HEADS, MLA_NOPE + MLA_V)
    wk = jnp.pad(wukv[:, :, :MLA_NOPE],
                 ((0, 0), (0, 0), (0, HEAD_PAD - MLA_NOPE))).reshape(KV_LORA, QK_PAD).astype(bf16)
    wvt = wukv[:, :, MLA_NOPE:].reshape(KV_LORA, MLA_WIDTH).T.astype(bf16)
    pad_g = lambda g: jnp.pad(g, (0, HEAD_PAD - MLA_QK)).reshape(1, HEAD_PAD)
    partner = jnp.concatenate([jnp.zeros((MLA_NOPE,), qk_q_g.dtype), qk_q_g[MLA_NOPE + half:],
                               qk_q_g[MLA_NOPE:MLA_NOPE + half]])
    gq2 = jnp.concatenate([pad_g(qk_q_g), pad_g(partner)], axis=0)
    sgbm = jnp.repeat(sg_b.T, SG_HEAD_DIM, axis=1)
    return win, wuq, wk, wvt, gq2, pad_g(qk_k_g), sgbm


def _const(shape):
    return pl.BlockSpec(shape, lambda *_: (0,) * len(shape), pipeline_mode=pl.Buffered(1))


def _normalise(x, g):
    n_tok = x.shape[0] * x.shape[1]
    tile = pl.BlockSpec((TK, D_MODEL), lambda i: (i, 0))
    return pl.pallas_call(
        _norm_kernel, grid=(n_tok // TK,),
        in_specs=[tile, _const((1, D_MODEL))], out_specs=tile,
        out_shape=jax.ShapeDtypeStruct((n_tok, D_MODEL), jnp.bfloat16),
        compiler_params=pltpu.CompilerParams(dimension_semantics=("arbitrary",),
                                             vmem_limit_bytes=VMEM_LIMIT),
        name="norm",
    )(x.reshape(n_tok, D_MODEL), g.reshape(1, D_MODEL))


def _layer(x, h, rope, p, next_norm_g):
    B, S, _ = x.shape
    n_tok = B * S
    tstep = FRONT_TILES * TM
    n_steps = n_tok // tstep
    steps_per_batch = S // tstep
    bf16 = jnp.bfloat16
    row = lambda a: a.reshape(1, -1)
    win, wuq, wk, wvt, gq, gk, sgbm = _prep_layer(
        p['w_in'], p['w_uq'], p['w_ukv'], p['qk_q_g'], p['qk_k_g'], p['sg_b'])
    bng = p['branch_norm_g']
    ropespec = pl.BlockSpec((tstep, HEAD_PAD), lambda s: (s % steps_per_batch, 0))
    front_in = [
        (h, pl.BlockSpec((tstep, D_MODEL), lambda s: (s, 0))),
        (h, pl.BlockSpec((TM, D_MODEL),
                         lambda s: (jnp.minimum(FRONT_TILES * (s + 1), n_tok // TM - 1), 0))),
        (win, _const((D_MODEL, IN_COLS_PAD))),
        (p['conv_w'], _const((CONV_K, CONV_WIDTH))),
        (row(p['conv_b']), _const((1, CONV_WIDTH))),
        (row(p['conv_ln_g']), _const((1, CONV_WIDTH))),
        (row(p['conv_ln_b']), _const((1, CONV_WIDTH))),
        (p['conv_pw_w'].astype(bf16), _const((CONV_WIDTH, CONV_WIDTH))),
        (row(p['conv_pw_b']), _const((1, CONV_WIDTH))),
        (row(p['q_norm_g']), _const((1, Q_LORA))),
        (wuq, _const((Q_LORA, QK_PAD))),
        (row(p['kv_norm_g']), _const((1, KV_LORA))),
        (wk, _const((KV_LORA, QK_PAD))),
        (wvt, _const((MLA_WIDTH, KV_LORA))),
        (gq, _const((2, HEAD_PAD))),
        (gk, _const((1, HEAD_PAD))),
        (row(p['sg_ln_g']), _const((1, SG_WIDTH))),
        (row(p['sg_ln_b']), _const((1, SG_WIDTH))),
        (p['sg_w'], _const((SG_HEADS, SG_CHUNK, SG_CHUNK))),
        (sgbm, _const((SG_CHUNK, SG_WIDTH))),
        (row(bng[:CONV_WIDTH]), _const((1, CONV_WIDTH))),
        (row(bng[CONV_WIDTH + MLA_WIDTH:]), _const((1, SG_WIDTH))),
        (rope[0], ropespec), (rope[1], ropespec), (rope[2], ropespec),
    ]
    otile = lambda w: pl.BlockSpec((tstep, w), lambda s: (s, 0))
    ycs, q, k, vt, zg = pl.pallas_call(
        functools.partial(_front_kernel, steps_per_batch=steps_per_batch),
        grid=(n_steps,),
        in_specs=[s for _, s in front_in],
        out_specs=[otile(CONV_WIDTH + SG_WIDTH), otile(QK_PAD), otile(QK_PAD),
                   pl.BlockSpec((tstep // TK, MLA_WIDTH, TK), lambda s: (s, 0, 0)),
                   otile(MLA_WIDTH)],
        out_shape=[jax.ShapeDtypeStruct((n_tok, CONV_WIDTH + SG_WIDTH), bf16),
                   jax.ShapeDtypeStruct((n_tok, QK_PAD), bf16),
                   jax.ShapeDtypeStruct((n_tok, QK_PAD), bf16),
                   jax.ShapeDtypeStruct((n_tok // TK, MLA_WIDTH, TK), bf16),
                   jax.ShapeDtypeStruct((n_tok, MLA_WIDTH), bf16)],
        scratch_shapes=[pltpu.VMEM((TM, IN_COLS_PAD), jnp.float32),
                        pltpu.VMEM((CONV_HALO + TM, CONV_WIDTH), jnp.float32),
                        pltpu.VMEM((SUBLANES - 1, CONV_HALO + TM, CONV_WIDTH), jnp.float32)],
        compiler_params=pltpu.CompilerParams(
            dimension_semantics=("arbitrary",), vmem_limit_bytes=VMEM_LIMIT),
        name="front",
    )(*[a for a, _ in front_in])

    nq, nkv = S // TQ, S // TK
    ot = pl.pallas_call(
        _attn_kernel,
        grid=(B, nq),
        in_specs=[pl.BlockSpec((None, TQ, QK_PAD), lambda b, i: (b, i, 0)),
                  pl.BlockSpec((None, S, QK_PAD), lambda b, i: (b, 0, 0),
                               pipeline_mode=pl.Buffered(1)),
                  pl.BlockSpec((None, nkv, MLA_WIDTH, TK), lambda b, i: (b, 0, 0, 0),
                               pipeline_mode=pl.Buffered(1))],
        out_specs=pl.BlockSpec((None, TQ, MLA_WIDTH), lambda b, i: (b, i, 0)),
        out_shape=jax.ShapeDtypeStruct((B, S, MLA_WIDTH), bf16),
        scratch_shapes=[pltpu.VMEM((MLA_HEADS, TK, TQ), jnp.float32)] * 2
                       + [pltpu.VMEM((MLA_HEADS, SUBLANES, TQ), jnp.float32)] * 3
                       + [pltpu.VMEM((MLA_HEADS, ACC_ROWS, TQ), jnp.float32)],
        compiler_params=pltpu.CompilerParams(
            dimension_semantics=("arbitrary", "arbitrary"), vmem_limit_bytes=VMEM_LIMIT),
        name="attn",
    )(q.reshape(B, S, QK_PAD), k.reshape(B, S, QK_PAD), vt.reshape(B, nkv, MLA_WIDTH, TK))

    tile = lambda w: pl.BlockSpec((None, TB, w), lambda b, i: (b, i, 0))
    back_in = [(x, tile(D_MODEL)), (ycs.reshape(B, S, -1), tile(CONV_WIDTH + SG_WIDTH)),
               (ot, tile(MLA_WIDTH)), (zg.reshape(B, S, -1), tile(MLA_WIDTH)),
               (row(bng[CONV_WIDTH:CONV_WIDTH + MLA_WIDTH]), _const((1, MLA_WIDTH))),
               (p['w_out'].astype(bf16), _const((D_MODEL, D_MODEL)))]
    out_specs = [tile(D_MODEL)]
    out_shape = [jax.ShapeDtypeStruct((B, S, D_MODEL), jnp.float32)]
    if next_norm_g is not None:
        back_in.append((row(next_norm_g), _const((1, D_MODEL))))
        out_specs.append(tile(D_MODEL))
        out_shape.append(jax.ShapeDtypeStruct((B, S, D_MODEL), bf16))
    outs = pl.pallas_call(
        _back_kernel,
        grid=(B, S // TB),
        in_specs=[s for _, s in back_in],
        out_specs=out_specs,
        out_shape=out_shape,
        compiler_params=pltpu.CompilerParams(
            dimension_semantics=("arbitrary", "arbitrary"), vmem_limit_bytes=VMEM_LIMIT),
        name="back",
    )(*[a for a, _ in back_in])
    if next_norm_g is None:
        return outs[0], None
    return outs[0], outs[1].reshape(n_tok, D_MODEL)


def kernel(x, norm_g, w_in, conv_w, conv_b, conv_ln_g, conv_ln_b, conv_pw_w, conv_pw_b,
           q_norm_g, w_uq, kv_norm_g, w_ukv, qk_q_g, qk_k_g, sg_ln_g, sg_ln_b, sg_w, sg_b,
           branch_norm_g, w_out):
    params = dict(norm_g=norm_g, w_in=w_in, conv_w=conv_w, conv_b=conv_b, conv_ln_g=conv_ln_g,
                  conv_ln_b=conv_ln_b, conv_pw_w=conv_pw_w, conv_pw_b=conv_pw_b,
                  q_norm_g=q_norm_g, w_uq=w_uq, kv_norm_g=kv_norm_g, w_ukv=w_ukv,
                  qk_q_g=qk_q_g, qk_k_g=qk_k_g, sg_ln_g=sg_ln_g, sg_ln_b=sg_ln_b, sg_w=sg_w,
                  sg_b=sg_b, branch_norm_g=branch_norm_g, w_out=w_out)
    rope = _rope_lane_tables(x.shape[1])
    depth = norm_g.shape[0]
    h = _normalise(x, norm_g[0])
    for layer in range(depth):
        next_g = norm_g[layer + 1] if layer + 1 < depth else None
        x, h = _layer(x, h, rope, {name: a[layer] for name, a in params.items()}, next_g)
    return x
```

```python
import functools

import jax
import jax.numpy as jnp
from jax import lax
from jax.experimental import pallas as pl
from jax.experimental.pallas import tpu as pltpu

D_MODEL = 1024
CONV_WIDTH = 256
CONV_K = 31
MLA_HEADS = 8
MLA_NOPE = 64
MLA_ROPE = 32
MLA_QK = MLA_NOPE + MLA_ROPE
MLA_V = 64
MLA_WIDTH = MLA_HEADS * MLA_V
Q_LORA = 768
KV_LORA = 256
ROPE_THETA = 10000.0
SG_WIDTH = 256
SG_HEADS = 4
SG_HEAD_DIM = SG_WIDTH // SG_HEADS
SG_CHUNK = 128
EPS = 1e-6

SUBLANES = 8
LANES = 128
HEAD_PAD = LANES
QK_PAD = MLA_HEADS * HEAD_PAD

OFF_A = 0
OFF_GLU = OFF_A + CONV_WIDTH
OFF_ZC = OFF_GLU + CONV_WIDTH
OFF_CQ = OFF_ZC + CONV_WIDTH
OFF_CKV = OFF_CQ + Q_LORA
OFF_KR = OFF_CKV + KV_LORA
OFF_ZM = OFF_KR + HEAD_PAD
OFF_U = OFF_ZM + MLA_WIDTH
OFF_V = OFF_U + SG_WIDTH
OFF_ZS = OFF_V + SG_WIDTH
IN_COLS_PAD = OFF_ZS + SG_WIDTH

CONV_HALO = 32
CONV_ROWS = 64
TM = 256
TB = 512
FRONT_TILES = 4
TQ = 256
TK = 512
ACC_ROWS = MLA_V + 16
NEG_BIG = -1e30
LOG2E = 1.4426950408889634
VMEM_LIMIT = 48 * 1024 * 1024

assert FRONT_TILES >= 2 and (FRONT_TILES * TM) % TK == 0 and TK % TM == 0 and TM % SG_CHUNK == 0 and TK == 2 * TQ


def _rms(x, g, width):
    ss = jnp.sum(x * x, axis=-1, keepdims=True)
    return x * lax.rsqrt(ss + width * EPS) * (g * width ** 0.5)


def _layer_norm(x, g, b, width):
    mu = jnp.sum(x, axis=-1, keepdims=True) * (1.0 / width)
    xc = x - mu
    ss = jnp.sum(xc * xc, axis=-1, keepdims=True)
    return xc * lax.rsqrt(ss + width * EPS) * (g * width ** 0.5) + b


def _sigmoid(x):
    return 0.5 * jnp.tanh(0.5 * x) + 0.5


def _silu(x):
    return x * _sigmoid(x)


def _rope(x, c, s_up, s_dn):
    return x * c + pltpu.roll(x, LANES - MLA_ROPE // 2, 1) * s_up + pltpu.roll(x, MLA_ROPE // 2, 1) * s_dn


def _front_kernel(h_ref, hnext_ref, ing_ref, win_ref, convw_ref, convb_ref, clng_ref,
                  clnb_ref, pww_ref, pwb_ref, qng_ref, wuq_ref, kvng_ref, wk_ref, wvt_ref,
                  gq_ref, gk_ref, sglng_ref, sglnb_ref, sgw_ref, sgbm_ref,
                  bgc_ref, bgs_ref, rc_ref, rsu_ref, rsd_ref,
                  ycs_ref, q_ref, k_ref, vt_ref, zg_ref,
                  proj0_ref, ybuf_ref, zs_ref, *, steps_per_batch, normalise_input):
    f32, bf16 = jnp.float32, jnp.bfloat16
    step = pl.program_id(0)

    def project(h_tile):
        if normalise_input:
            h_tile = _rms(h_tile, ing_ref[...], D_MODEL).astype(bf16)
        return jnp.dot(h_tile, win_ref[...], preferred_element_type=f32)

    tile_rows = lambda t: h_ref[t * TM:(t + 1) * TM, :]

    def up_project(proj, t):
        cq = _rms(proj[:, OFF_CQ:OFF_CQ + Q_LORA], qng_ref[...], Q_LORA).astype(bf16)
        qf = jnp.dot(cq, wuq_ref[...], preferred_element_type=f32)
        ckv = _rms(proj[:, OFF_CKV:OFF_CKV + KV_LORA], kvng_ref[...], KV_LORA).astype(bf16)
        kf = jnp.dot(ckv, wk_ref[...], preferred_element_type=f32)
        vt = lax.dot_general(wvt_ref[...], ckv, (((1,), (1,)), ((), ())),
                             preferred_element_type=f32)
        vt_ref[t * TM // TK, :, (t * TM) % TK:(t * TM) % TK + TM] = vt.astype(bf16)
        return qf, kf

    def mix(proj, qf, kf, t):
        rows = slice(t * TM, (t + 1) * TM)
        col = lambda off, w: proj[:, off:off + w]

        v = _layer_norm(jax.nn.gelu(col(OFF_V, SG_WIDTH)), sglng_ref[...], sglnb_ref[...], SG_WIDTH)
        trow = lax.broadcasted_iota(jnp.int32, (SG_CHUNK, SG_CHUNK), 0)
        tcol = lax.broadcasted_iota(jnp.int32, (SG_CHUNK, SG_CHUNK), 1)
        wcat = jnp.concatenate(
            [jnp.where(tcol <= trow, sgw_ref[g], 0.0) for g in range(SG_HEADS)], axis=1).astype(bf16)
        lane = lax.broadcasted_iota(jnp.int32, (SG_CHUNK, SG_WIDTH), 1)
        mixed = []
        for c in range(TM // SG_CHUNK):
            vc = v[c * SG_CHUNK:(c + 1) * SG_CHUNK, :]
            vstack = jnp.concatenate(
                [jnp.where((lane >= g * SG_HEAD_DIM) & (lane < (g + 1) * SG_HEAD_DIM), vc, 0.0)
                 for g in range(SG_HEADS)], axis=0).astype(bf16)
            mixed.append(jnp.dot(wcat, vstack, preferred_element_type=f32) + sgbm_ref[...])
        ys = jax.nn.gelu(col(OFF_U, SG_WIDTH)) * jnp.concatenate(mixed, axis=0) * _silu(col(OFF_ZS, SG_WIDTH))
        ycs_ref[rows, CONV_WIDTH:CONV_WIDTH + SG_WIDTH] = _rms(ys, bgs_ref[...], SG_WIDTH).astype(bf16)

        ybuf_ref[CONV_HALO:CONV_HALO + TM, :] = col(OFF_A, CONV_WIDTH) * _sigmoid(col(OFF_GLU, CONV_WIDTH))
        first = CONV_HALO - (CONV_K - 1)
        for r in range(1, SUBLANES):
            zs_ref[r - 1, 0:TM + CONV_HALO - SUBLANES, :] = ybuf_ref[r:r + TM + CONV_HALO - SUBLANES, :]
        chunks = []
        for c0 in range(0, TM, CONV_ROWS):
            conv = jnp.zeros((CONV_ROWS, CONV_WIDTH), f32) + convb_ref[...]
            for tap in range(CONV_K):
                r, off = (tap + first) % SUBLANES, (tap + first) // SUBLANES * SUBLANES + c0
                src = ybuf_ref if r == 0 else zs_ref.at[r - 1]
                conv = conv + src[off:off + CONV_ROWS, :] * convw_ref[tap:tap + 1, :]
            chunks.append(_silu(_layer_norm(conv, clng_ref[...], clnb_ref[...], CONV_WIDTH)).astype(bf16))
        ybuf_ref[0:CONV_HALO, :] = ybuf_ref[TM:TM + CONV_HALO, :]
        yc = jnp.concatenate(chunks, axis=0)
        yc = jnp.dot(yc, pww_ref[...], preferred_element_type=f32) + pwb_ref[...]
        yc = yc * _silu(col(OFF_ZC, CONV_WIDTH))
        ycs_ref[rows, 0:CONV_WIDTH] = _rms(yc, bgc_ref[...], CONV_WIDTH).astype(bf16)

        zg_ref[rows, :] = _silu(col(OFF_ZM, MLA_WIDTH)).astype(bf16)
        rc, rsu, rsd = rc_ref[rows, :], rsu_ref[rows, :], rsd_ref[rows, :]
        qc = rc * (gq_ref[0:1, :] * LOG2E)
        qs = (rsu + rsd) * (gq_ref[1:2, :] * LOG2E)
        real = (lax.broadcasted_iota(jnp.int32, (1, HEAD_PAD), 1) < MLA_QK).astype(f32)
        for hd in range(MLA_HEADS):
            qh = qf[:, hd * HEAD_PAD:(hd + 1) * HEAD_PAD]
            ss = jnp.sum(qh * qh * real, axis=-1, keepdims=True)
            roped = qh * qc + pltpu.roll(qh, LANES - MLA_ROPE, 1) * qs
            q_ref[rows, hd * HEAD_PAD:(hd + 1) * HEAD_PAD] = (
                roped * lax.rsqrt(ss + MLA_QK * EPS)).astype(bf16)
        gk = gk_ref[...] * MLA_QK ** 0.5
        kr = col(OFF_KR, HEAD_PAD)
        ss_r = jnp.sum(kr * kr, axis=-1, keepdims=True)
        krr = _rope(kr * gk, rc, rsu, rsd)
        for hd in range(MLA_HEADS):
            kh = kf[:, hd * HEAD_PAD:(hd + 1) * HEAD_PAD]
            ss = jnp.sum(kh * kh, axis=-1, keepdims=True) + ss_r
            k_ref[rows, hd * HEAD_PAD:(hd + 1) * HEAD_PAD] = (
                (kh * gk + krr) * lax.rsqrt(ss + MLA_QK * EPS)).astype(bf16)

    @pl.when(step % steps_per_batch == 0)
    def _():
        ybuf_ref[0:CONV_HALO, :] = jnp.zeros((CONV_HALO, CONV_WIDTH), f32)

    @pl.when(step == 0)
    def _():
        proj0_ref[...] = project(tile_rows(0))

    projs = {0: proj0_ref, 1: project(tile_rows(1))}
    for t in range(FRONT_TILES):
        proj = projs.pop(t)
        qf, kf = up_project(proj, t)
        if t + 2 < FRONT_TILES:
            projs[t + 2] = project(tile_rows(t + 2))
        mix(proj, qf, kf, t)
    proj0_ref[...] = project(hnext_ref[...])


def _col_max8(s):
    m8 = jnp.max(s.reshape(s.shape[0] // SUBLANES, SUBLANES, s.shape[1]), axis=0)
    for shift in (4, 2, 1):
        m8 = jnp.maximum(m8, pltpu.roll(m8, shift, 0))
    return m8


def _rows8(x, fn, stat8):
    x3 = x.reshape(x.shape[0] // SUBLANES, SUBLANES, x.shape[1])
    return fn(x3, stat8[None]).reshape(x.shape)


def _attn_kernel(q_ref, k_ref, vt_ref, o_ref, sa_ref, sb_ref, maxa_ref, maxb_ref, m_ref, acc_ref):
    f32, bf16 = jnp.float32, jnp.bfloat16
    qi = pl.program_id(1)
    n_full = qi // (TK // TQ)
    diag_shift = (qi % (TK // TQ)) * TQ
    key_minus_query = (lax.broadcasted_iota(jnp.int32, (TK, TQ), 0)
                       - lax.broadcasted_iota(jnp.int32, (TK, TQ), 1))
    ones_rows = jnp.ones((ACC_ROWS - MLA_V, TK), bf16)
    heads = range(MLA_HEADS)
    hslab = lambda hh: slice(hh * HEAD_PAD, (hh + 1) * HEAD_PAD)
    vslab = lambda hh: slice(hh * MLA_V, (hh + 1) * MLA_V)

    def scores(j, s_ref, max_ref, hh):
        kb = k_ref[pl.ds(pl.multiple_of(j * TK, TK), TK), hslab(hh)]
        s = lax.dot_general(kb, q_ref[:, hslab(hh)], (((1,), (1,)), ((), ())),
                            preferred_element_type=f32)
        s_ref[hh] = s
        max_ref[hh] = _col_max8(s)

    def consume(j, s_ref, max_ref, hh, masked):
        s = s_ref[hh]
        if masked:
            s = jnp.where(key_minus_query <= diag_shift, s, NEG_BIG)
            blk_max = _col_max8(s)
        else:
            blk_max = max_ref[hh]
        m = m_ref[hh]
        m_new = jnp.maximum(m, blk_max)
        alpha = jnp.exp2(m - m_new)
        p = _rows8(s, lambda x, st: jnp.exp2(x - st), m_new)
        m_ref[hh] = m_new
        lhs = jnp.concatenate([vt_ref[j, vslab(hh), :], ones_rows], axis=0)
        pv = jnp.dot(lhs, p.astype(bf16), preferred_element_type=f32)
        acc_ref[hh] = _rows8(acc_ref[hh], lambda x, st: x * st, alpha) + pv

    def half_step(j, cur, nxt):
        for hh in heads:
            scores(j + 1, nxt[0], nxt[1], hh)
            consume(j, cur[0], cur[1], hh, False)

    buf_a, buf_b = (sa_ref, maxa_ref), (sb_ref, maxb_ref)
    m_ref[...] = jnp.full(m_ref.shape, NEG_BIG, f32)
    acc_ref[...] = jnp.zeros(acc_ref.shape, f32)
    for hh in heads:
        scores(0, sa_ref, maxa_ref, hh)

    def pair(jj, carry):
        half_step(2 * jj, buf_a, buf_b)
        half_step(2 * jj + 1, buf_b, buf_a)
        return carry

    lax.fori_loop(0, n_full // 2, pair, 0)

    @pl.when(n_full % 2 == 1)
    def _():
        half_step(n_full - 1, buf_a, buf_b)
        for hh in heads:
            consume(n_full, sb_ref, maxb_ref, hh, True)

    @pl.when(n_full % 2 == 0)
    def _():
        for hh in heads:
            consume(n_full, sa_ref, maxa_ref, hh, True)

    out_t = jnp.concatenate(
        [_rows8(acc_ref[hh][:MLA_V], lambda x, st: x / st, acc_ref[hh][MLA_V:MLA_V + SUBLANES])
         for hh in heads], axis=0)
    o_ref[...] = out_t.T.astype(bf16)


def _back_kernel(x_ref, ycs_ref, ot_ref, zg_ref, bgm_ref, wout_ref, *rest):
    f32, bf16 = jnp.float32, jnp.bfloat16
    ym = ot_ref[...].astype(f32) * zg_ref[...].astype(f32)
    ym = _rms(ym, bgm_ref[...], MLA_WIDTH).astype(bf16)
    ycs = ycs_ref[...]
    y = jnp.concatenate([ycs[:, 0:CONV_WIDTH], ym, ycs[:, CONV_WIDTH:]], axis=1)
    x_new = x_ref[...] + jnp.dot(y, wout_ref[...], preferred_element_type=f32)
    if len(rest) == 1:
        rest[0][...] = x_new
    else:
        next_g_ref, out_ref, next_h_ref = rest
        out_ref[...] = x_new
        next_h_ref[...] = _rms(x_new, next_g_ref[...], D_MODEL).astype(bf16)


def _rope_lane_tables(seq):
    half = MLA_ROPE // 2
    inv_freq = ROPE_THETA ** (-jnp.arange(half, dtype=jnp.float32) / half)
    ang = jnp.arange(seq, dtype=jnp.float32)[:, None] * inv_freq[None, :]
    cos, sin = jnp.cos(ang), jnp.sin(ang)
    ones = jnp.ones((seq, MLA_NOPE), jnp.float32)
    zpad = jnp.zeros((seq, HEAD_PAD - MLA_QK), jnp.float32)
    zn = jnp.zeros((seq, MLA_NOPE), jnp.float32)
    zh = jnp.zeros((seq, half), jnp.float32)
    c = jnp.concatenate([ones, cos, cos, zpad], axis=1)
    s_up = jnp.concatenate([zn, -sin, zh, zpad], axis=1)
    s_dn = jnp.concatenate([zn, zh, sin, zpad], axis=1)
    return c, s_up, s_dn


def _prep_params(p):
    bf16 = jnp.bfloat16
    depth = p['w_in'].shape[0]
    half = MLA_ROPE // 2
    w_in = p['w_in']
    zc = lambda n: jnp.zeros((depth, D_MODEL, n), bf16)
    kr0 = 3 * CONV_WIDTH + Q_LORA + KV_LORA
    win = jnp.concatenate(
        [w_in[:, :, :kr0].astype(bf16), zc(MLA_NOPE), w_in[:, :, kr0:kr0 + MLA_ROPE].astype(bf16),
         zc(HEAD_PAD - MLA_QK), w_in[:, :, kr0 + MLA_ROPE:].astype(bf16)], axis=2)
    wq = p['w_uq'].astype(bf16).reshape(depth, Q_LORA, MLA_HEADS, MLA_QK)
    wuq = jnp.concatenate([wq, wq[..., MLA_NOPE + half:], wq[..., MLA_NOPE:MLA_NOPE + half]],
                          axis=3).reshape(depth, Q_LORA, QK_PAD)
    wukv = p['w_ukv'].astype(bf16).reshape(depth, KV_LORA, MLA_HEADS, MLA_NOPE + MLA_V)
    wk = jnp.pad(wukv[..., :MLA_NOPE], ((0, 0), (0, 0), (0, 0), (0, HEAD_PAD - MLA_NOPE))
                 ).reshape(depth, KV_LORA, QK_PAD)
    wvt = jnp.swapaxes(wukv[..., MLA_NOPE:].reshape(depth, KV_LORA, MLA_WIDTH), 1, 2)
    pad_g = lambda g: jnp.pad(g, ((0, 0), (0, HEAD_PAD - g.shape[1])))[:, None, :]
    gq = p['qk_q_g']
    partner = jnp.concatenate([jnp.zeros((depth, MLA_NOPE), gq.dtype), gq[:, MLA_NOPE + half:],
                               gq[:, MLA_NOPE:MLA_NOPE + half]], axis=1)
    bng = p['branch_norm_g']
    vec = lambda a: a[:, None, :]
    return dict(
        win=win, wuq=wuq, wk=wk, wvt=wvt,
        gq=jnp.concatenate([pad_g(gq), pad_g(partner)], axis=1), gk=pad_g(p['qk_k_g']),
        sgbm=jnp.repeat(jnp.swapaxes(p['sg_b'], 1, 2), SG_HEAD_DIM, axis=2),
        pww=p['conv_pw_w'].astype(bf16), wout=p['w_out'].astype(bf16),
        conv_w=p['conv_w'], sg_w=p['sg_w'],
        conv_b=vec(p['conv_b']), conv_ln_g=vec(p['conv_ln_g']), conv_ln_b=vec(p['conv_ln_b']),
        conv_pw_b=vec(p['conv_pw_b']), q_norm_g=vec(p['q_norm_g']), kv_norm_g=vec(p['kv_norm_g']),
        sg_ln_g=vec(p['sg_ln_g']), sg_ln_b=vec(p['sg_ln_b']), norm_g=vec(p['norm_g']),
        bg_conv=vec(bng[:, :CONV_WIDTH]), bg_mla=vec(bng[:, CONV_WIDTH:CONV_WIDTH + MLA_WIDTH]),
        bg_sg=vec(bng[:, CONV_WIDTH + MLA_WIDTH:]))


def _layer(x, h, rope, w, layer, emit_next_h):
    B, S, _ = x.shape
    n_tok = B * S
    tstep = FRONT_TILES * TM
    n_steps = n_tok // tstep
    steps_per_batch = S // tstep
    bf16 = jnp.bfloat16

    def of_layer(name, which=layer):
        shape = w[name].shape[1:]
        return w[name], pl.BlockSpec((None,) + shape, lambda *_: (which,) + (0,) * len(shape),
                                     pipeline_mode=pl.Buffered(1))

    ropespec = pl.BlockSpec((tstep, HEAD_PAD), lambda s: (s % steps_per_batch, 0))
    front_in = [
        (h, pl.BlockSpec((tstep, D_MODEL), lambda s: (s, 0))),
        (h, pl.BlockSpec((TM, D_MODEL),
                         lambda s: (jnp.minimum(FRONT_TILES * (s + 1), n_tok // TM - 1), 0))),
        of_layer('norm_g'), of_layer('win'), of_layer('conv_w'), of_layer('conv_b'), of_layer('conv_ln_g'),
        of_layer('conv_ln_b'), of_layer('pww'), of_layer('conv_pw_b'), of_layer('q_norm_g'),
        of_layer('wuq'), of_layer('kv_norm_g'), of_layer('wk'), of_layer('wvt'),
        of_layer('gq'), of_layer('gk'), of_layer('sg_ln_g'), of_layer('sg_ln_b'),
        of_layer('sg_w'), of_layer('sgbm'), of_layer('bg_conv'), of_layer('bg_sg'),
        (rope[0], ropespec), (rope[1], ropespec), (rope[2], ropespec),
    ]
    otile = lambda width: pl.BlockSpec((tstep, width), lambda s: (s, 0))
    ycs, q, k, vt, zg = pl.pallas_call(
        functools.partial(_front_kernel, steps_per_batch=steps_per_batch,
                          normalise_input=h.dtype != bf16),
        grid=(n_steps,),
        in_specs=[s for _, s in front_in],
        out_specs=[otile(CONV_WIDTH + SG_WIDTH), otile(QK_PAD), otile(QK_PAD),
                   pl.BlockSpec((tstep // TK, MLA_WIDTH, TK), lambda s: (s, 0, 0)),
                   otile(MLA_WIDTH)],
        out_shape=[jax.ShapeDtypeStruct((n_tok, CONV_WIDTH + SG_WIDTH), bf16),
                   jax.ShapeDtypeStruct((n_tok, QK_PAD), bf16),
                   jax.ShapeDtypeStruct((n_tok, QK_PAD), bf16),
                   jax.ShapeDtypeStruct((n_tok // TK, MLA_WIDTH, TK), bf16),
                   jax.ShapeDtypeStruct((n_tok, MLA_WIDTH), bf16)],
        scratch_shapes=[pltpu.VMEM((TM, IN_COLS_PAD), jnp.float32),
                        pltpu.VMEM((CONV_HALO + TM, CONV_WIDTH), jnp.float32),
                        pltpu.VMEM((SUBLANES - 1, CONV_HALO + TM, CONV_WIDTH), jnp.float32)],
        compiler_params=pltpu.CompilerParams(
            dimension_semantics=("arbitrary",), vmem_limit_bytes=VMEM_LIMIT),
        name="front",
    )(*[a for a, _ in front_in])

    nq, nkv = S // TQ, S // TK
    ot = pl.pallas_call(
        _attn_kernel,
        grid=(B, nq),
        in_specs=[pl.BlockSpec((None, TQ, QK_PAD), lambda b, i: (b, i, 0)),
                  pl.BlockSpec((None, S, QK_PAD), lambda b, i: (b, 0, 0),
                               pipeline_mode=pl.Buffered(1)),
                  pl.BlockSpec((None, nkv, MLA_WIDTH, TK), lambda b, i: (b, 0, 0, 0),
                               pipeline_mode=pl.Buffered(1))],
        out_specs=pl.BlockSpec((None, TQ, MLA_WIDTH), lambda b, i: (b, i, 0)),
        out_shape=jax.ShapeDtypeStruct((B, S, MLA_WIDTH), bf16),
        scratch_shapes=[pltpu.VMEM((MLA_HEADS, TK, TQ), jnp.float32)] * 2
                       + [pltpu.VMEM((MLA_HEADS, SUBLANES, TQ), jnp.float32)] * 3
                       + [pltpu.VMEM((MLA_HEADS, ACC_ROWS, TQ), jnp.float32)],
        compiler_params=pltpu.CompilerParams(
            dimension_semantics=("arbitrary", "arbitrary"), vmem_limit_bytes=VMEM_LIMIT),
        name="attn",
    )(q.reshape(B, S, QK_PAD), k.reshape(B, S, QK_PAD), vt.reshape(B, nkv, MLA_WIDTH, TK))

    tile = lambda width: pl.BlockSpec((None, TB, width), lambda b, i: (b, i, 0))
    back_in = [(x, tile(D_MODEL)), (ycs.reshape(B, S, -1), tile(CONV_WIDTH + SG_WIDTH)),
               (ot, tile(MLA_WIDTH)), (zg.reshape(B, S, -1), tile(MLA_WIDTH)),
               of_layer('bg_mla'), of_layer('wout')]
    out_specs = [tile(D_MODEL)]
    out_shape = [jax.ShapeDtypeStruct((B, S, D_MODEL), jnp.float32)]
    if emit_next_h:
        back_in.append(of_layer('norm_g', layer + 1))
        out_specs.append(tile(D_MODEL))
        out_shape.append(jax.ShapeDtypeStruct((B, S, D_MODEL), bf16))
    outs = pl.pallas_call(
        _back_kernel,
        grid=(B, S // TB),
        in_specs=[s for _, s in back_in],
        out_specs=out_specs,
        out_shape=out_shape,
        compiler_params=pltpu.CompilerParams(
            dimension_semantics=("arbitrary", "arbitrary"), vmem_limit_bytes=VMEM_LIMIT),
        name="back",
    )(*[a for a, _ in back_in])
    return outs[0], (outs[1].reshape(n_tok, D_MODEL) if emit_next_h else None)


def kernel(x, norm_g, w_in, conv_w, conv_b, conv_ln_g, conv_ln_b, conv_pw_w, conv_pw_b,
           q_norm_g, w_uq, kv_norm_g, w_ukv, qk_q_g, qk_k_g, sg_ln_g, sg_ln_b, sg_w, sg_b,
           branch_norm_g, w_out):
    params = dict(norm_g=norm_g, w_in=w_in, conv_w=conv_w, conv_b=conv_b, conv_ln_g=conv_ln_g,
                  conv_ln_b=conv_ln_b, conv_pw_w=conv_pw_w, conv_pw_b=conv_pw_b,
                  q_norm_g=q_norm_g, w_uq=w_uq, kv_norm_g=kv_norm_g, w_ukv=w_ukv,
                  qk_q_g=qk_q_g, qk_k_g=qk_k_g, sg_ln_g=sg_ln_g, sg_ln_b=sg_ln_b, sg_w=sg_w,
                  sg_b=sg_b, branch_norm_g=branch_norm_g, w_out=w_out)
    rope = _rope_lane_tables(x.shape[1])
    depth = norm_g.shape[0]
    w = _prep_params(params)
    h = x.reshape(-1, D_MODEL)
    for layer in range(depth):
        x, h = _layer(x, h, rope, w, layer, layer + 1 < depth)
    return x
```

```python
import functools

import jax
import jax.numpy as jnp
from jax import lax
from jax.experimental import pallas as pl
from jax.experimental.pallas import tpu as pltpu

D_MODEL = 1024
CONV_WIDTH = 256
CONV_K = 31
MLA_HEADS = 8
MLA_NOPE = 64
MLA_ROPE = 32
MLA_QK = MLA_NOPE + MLA_ROPE
MLA_V = 64
MLA_WIDTH = MLA_HEADS * MLA_V
Q_LORA = 768
KV_LORA = 256
ROPE_THETA = 10000.0
SG_WIDTH = 256
SG_HEADS = 4
SG_HEAD_DIM = SG_WIDTH // SG_HEADS
SG_CHUNK = 128
EPS = 1e-6

SUBLANES = 8
LANES = 128
HEAD_PAD = LANES
QK_PAD = MLA_HEADS * HEAD_PAD

OFF_A = 0
OFF_GLU = OFF_A + CONV_WIDTH
OFF_ZC = OFF_GLU + CONV_WIDTH
OFF_CQ = OFF_ZC + CONV_WIDTH
OFF_CKV = OFF_CQ + Q_LORA
OFF_KR = OFF_CKV + KV_LORA
OFF_ZM = OFF_KR + HEAD_PAD
OFF_U = OFF_ZM + MLA_WIDTH
OFF_V = OFF_U + SG_WIDTH
OFF_ZS = OFF_V + SG_WIDTH
IN_COLS_PAD = OFF_ZS + SG_WIDTH

CONV_HALO = 32
CONV_ROWS = 64
TM = 256
TB = 512
FRONT_TILES = 4
TQ = 256
TK = 512
ACC_ROWS = MLA_V + 16
NEG_BIG = -1e30
LOG2E = 1.4426950408889634
VMEM_LIMIT = 48 * 1024 * 1024

assert FRONT_TILES >= 2 and (FRONT_TILES * TM) % TK == 0 and TK % TM == 0 and TM % SG_CHUNK == 0 and TK == 2 * TQ


def _rms(x, g, width):
    ss = jnp.sum(x * x, axis=-1, keepdims=True)
    return x * lax.rsqrt(ss + width * EPS) * (g * width ** 0.5)


def _layer_norm(x, g, b, width):
    mu = jnp.sum(x, axis=-1, keepdims=True) * (1.0 / width)
    xc = x - mu
    ss = jnp.sum(xc * xc, axis=-1, keepdims=True)
    return xc * lax.rsqrt(ss + width * EPS) * (g * width ** 0.5) + b


def _sigmoid(x):
    return 0.5 * jnp.tanh(0.5 * x) + 0.5


def _silu(x):
    return x * _sigmoid(x)


def _rope(x, c, s_up, s_dn):
    return x * c + pltpu.roll(x, LANES - MLA_ROPE // 2, 1) * s_up + pltpu.roll(x, MLA_ROPE // 2, 1) * s_dn


def _front_kernel(h_ref, hnext_ref, ing_ref, win_ref, convw_ref, convb_ref, clng_ref,
                  clnb_ref, pww_ref, pwb_ref, qng_ref, wuq_ref, kvng_ref, wk_ref, wvt_ref,
                  gq_ref, gk_ref, sglng_ref, sglnb_ref, sgw_ref, sgbm_ref,
                  rc_ref, rsu_ref, rsd_ref,
                  ycs_ref, q_ref, k_ref, vt_ref, zm_ref,
                  proj0_ref, ybuf_ref, zs_ref, *, steps_per_batch, normalise_input):
    f32, bf16 = jnp.float32, jnp.bfloat16
    step = pl.program_id(0)

    def project(h_tile):
        if normalise_input:
            h_tile = _rms(h_tile, ing_ref[...], D_MODEL).astype(bf16)
        return jnp.dot(h_tile, win_ref[...], preferred_element_type=f32)

    tile_rows = lambda t: h_ref[t * TM:(t + 1) * TM, :]

    def up_project(proj, t):
        cq = _rms(proj[:, OFF_CQ:OFF_CQ + Q_LORA], qng_ref[...], Q_LORA).astype(bf16)
        qf = jnp.dot(cq, wuq_ref[...], preferred_element_type=f32)
        ckv = _rms(proj[:, OFF_CKV:OFF_CKV + KV_LORA], kvng_ref[...], KV_LORA).astype(bf16)
        kf = jnp.dot(ckv, wk_ref[...], preferred_element_type=f32)
        vt = lax.dot_general(wvt_ref[...], ckv, (((1,), (1,)), ((), ())),
                             preferred_element_type=f32)
        vt_ref[t * TM // TK, :, (t * TM) % TK:(t * TM) % TK + TM] = vt.astype(bf16)
        return qf, kf

    def mix(proj, qf, kf, t):
        rows = slice(t * TM, (t + 1) * TM)
        col = lambda off, w: proj[:, off:off + w]

        v = _layer_norm(jax.nn.gelu(col(OFF_V, SG_WIDTH)), sglng_ref[...], sglnb_ref[...], SG_WIDTH)
        trow = lax.broadcasted_iota(jnp.int32, (SG_CHUNK, SG_CHUNK), 0)
        tcol = lax.broadcasted_iota(jnp.int32, (SG_CHUNK, SG_CHUNK), 1)
        wcat = jnp.concatenate(
            [jnp.where(tcol <= trow, sgw_ref[g], 0.0) for g in range(SG_HEADS)], axis=1).astype(bf16)
        lane = lax.broadcasted_iota(jnp.int32, (SG_CHUNK, SG_WIDTH), 1)
        mixed = []
        for c in range(TM // SG_CHUNK):
            vc = v[c * SG_CHUNK:(c + 1) * SG_CHUNK, :]
            vstack = jnp.concatenate(
                [jnp.where((lane >= g * SG_HEAD_DIM) & (lane < (g + 1) * SG_HEAD_DIM), vc, 0.0)
                 for g in range(SG_HEADS)], axis=0).astype(bf16)
            mixed.append(jnp.dot(wcat, vstack, preferred_element_type=f32) + sgbm_ref[...])
        ys = jax.nn.gelu(col(OFF_U, SG_WIDTH)) * jnp.concatenate(mixed, axis=0) * _silu(col(OFF_ZS, SG_WIDTH))
        ycs_ref[rows, CONV_WIDTH:CONV_WIDTH + SG_WIDTH] = ys.astype(bf16)

        ybuf_ref[CONV_HALO:CONV_HALO + TM, :] = col(OFF_A, CONV_WIDTH) * _sigmoid(col(OFF_GLU, CONV_WIDTH))
        first = CONV_HALO - (CONV_K - 1)
        for r in range(1, SUBLANES):
            zs_ref[r - 1, 0:TM + CONV_HALO - SUBLANES, :] = ybuf_ref[r:r + TM + CONV_HALO - SUBLANES, :]
        chunks = []
        for c0 in range(0, TM, CONV_ROWS):
            conv = jnp.zeros((CONV_ROWS, CONV_WIDTH), f32) + convb_ref[...]
            for tap in range(CONV_K):
                r, off = (tap + first) % SUBLANES, (tap + first) // SUBLANES * SUBLANES + c0
                src = ybuf_ref if r == 0 else zs_ref.at[r - 1]
                conv = conv + src[off:off + CONV_ROWS, :] * convw_ref[tap:tap + 1, :]
            chunks.append(_silu(_layer_norm(conv, clng_ref[...], clnb_ref[...], CONV_WIDTH)).astype(bf16))
        ybuf_ref[0:CONV_HALO, :] = ybuf_ref[TM:TM + CONV_HALO, :]
        yc = jnp.concatenate(chunks, axis=0)
        yc = jnp.dot(yc, pww_ref[...], preferred_element_type=f32) + pwb_ref[...]
        yc = yc * _silu(col(OFF_ZC, CONV_WIDTH))
        ycs_ref[rows, 0:CONV_WIDTH] = yc.astype(bf16)

        zm_ref[rows, :] = col(OFF_ZM, MLA_WIDTH).astype(bf16)
        rc, rsu, rsd = rc_ref[rows, :], rsu_ref[rows, :], rsd_ref[rows, :]
        qc = rc * (gq_ref[0:1, :] * LOG2E)
        qs = (rsu + rsd) * (gq_ref[1:2, :] * LOG2E)
        real = (lax.broadcasted_iota(jnp.int32, (1, HEAD_PAD), 1) < MLA_QK).astype(f32)
        for hd in range(MLA_HEADS):
            qh = qf[:, hd * HEAD_PAD:(hd + 1) * HEAD_PAD]
            ss = jnp.sum(qh * qh * real, axis=-1, keepdims=True)
            roped = qh * qc + pltpu.roll(qh, LANES - MLA_ROPE, 1) * qs
            q_ref[rows, hd * HEAD_PAD:(hd + 1) * HEAD_PAD] = (
                roped * lax.rsqrt(ss + MLA_QK * EPS)).astype(bf16)
        gk = gk_ref[...] * MLA_QK ** 0.5
        kr = col(OFF_KR, HEAD_PAD)
        ss_r = jnp.sum(kr * kr, axis=-1, keepdims=True)
        krr = _rope(kr * gk, rc, rsu, rsd)
        for hd in range(MLA_HEADS):
            kh = kf[:, hd * HEAD_PAD:(hd + 1) * HEAD_PAD]
            ss = jnp.sum(kh * kh, axis=-1, keepdims=True) + ss_r
            k_ref[rows, hd * HEAD_PAD:(hd + 1) * HEAD_PAD] = (
                (kh * gk + krr) * lax.rsqrt(ss + MLA_QK * EPS)).astype(bf16)

    @pl.when(step % steps_per_batch == 0)
    def _():
        ybuf_ref[0:CONV_HALO, :] = jnp.zeros((CONV_HALO, CONV_WIDTH), f32)

    @pl.when(step == 0)
    def _():
        proj0_ref[...] = project(tile_rows(0))

    projs = {0: proj0_ref, 1: project(tile_rows(1))}
    for t in range(FRONT_TILES):
        proj = projs.pop(t)
        qf, kf = up_project(proj, t)
        if t + 2 < FRONT_TILES:
            projs[t + 2] = project(tile_rows(t + 2))
        mix(proj, qf, kf, t)
    proj0_ref[...] = project(hnext_ref[...])


def _col_max8(s):
    m8 = jnp.max(s.reshape(s.shape[0] // SUBLANES, SUBLANES, s.shape[1]), axis=0)
    for shift in (4, 2, 1):
        m8 = jnp.maximum(m8, pltpu.roll(m8, shift, 0))
    return m8


def _rows8(x, fn, stat8):
    x3 = x.reshape(x.shape[0] // SUBLANES, SUBLANES, x.shape[1])
    return fn(x3, stat8[None]).reshape(x.shape)


def _attn_kernel(q_ref, k_ref, vt_ref, o_ref, sa_ref, sb_ref, maxa_ref, maxb_ref, m_ref, acc_ref):
    f32, bf16 = jnp.float32, jnp.bfloat16
    qi = pl.program_id(1)
    n_full = qi // (TK // TQ)
    causal = (lax.broadcasted_iota(jnp.int32, (TQ, TQ), 0)
              <= lax.broadcasted_iota(jnp.int32, (TQ, TQ), 1))
    ones_rows = jnp.ones((ACC_ROWS - MLA_V, TK), bf16)
    heads = range(MLA_HEADS)
    hslab = lambda hh: slice(hh * HEAD_PAD, (hh + 1) * HEAD_PAD)
    vslab = lambda hh: slice(hh * MLA_V, (hh + 1) * MLA_V)

    def scores(j, s_ref, max_ref, hh):
        kb = k_ref[pl.ds(pl.multiple_of(j * TK, TK), TK), hslab(hh)]
        s = lax.dot_general(kb, q_ref[:, hslab(hh)], (((1,), (1,)), ((), ())),
                            preferred_element_type=f32)
        s_ref[hh] = s
        max_ref[hh] = _col_max8(s)

    def consume(j, s_ref, max_ref, hh, mode):
        if mode == "full":
            s, blk_max = s_ref[hh], max_ref[hh]
        elif mode == "diag_short":
            s = jnp.where(causal, s_ref[hh, 0:TQ, :], NEG_BIG)
            blk_max = _col_max8(s)
        else:
            s = jnp.concatenate([s_ref[hh, 0:TQ, :],
                                 jnp.where(causal, s_ref[hh, TQ:TK, :], NEG_BIG)], axis=0)
            blk_max = _col_max8(s)
        keys = s.shape[0]
        m = m_ref[hh]
        m_new = jnp.maximum(m, blk_max)
        alpha = jnp.exp2(m - m_new)
        p = _rows8(s, lambda x, st: jnp.exp2(x - st), m_new)
        m_ref[hh] = m_new
        lhs = jnp.concatenate([vt_ref[j, vslab(hh), 0:keys], ones_rows[:, 0:keys]], axis=0)
        pv = jnp.dot(lhs, p.astype(bf16), preferred_element_type=f32)
        acc_ref[hh] = _rows8(acc_ref[hh], lambda x, st: x * st, alpha) + pv

    def half_step(j, cur, nxt):
        for hh in heads:
            scores(j + 1, nxt[0], nxt[1], hh)
            consume(j, cur[0], cur[1], hh, "full")

    def diagonal(buf):
        @pl.when(qi % (TK // TQ) == 0)
        def _():
            for hh in heads:
                consume(n_full, buf[0], buf[1], hh, "diag_short")

        @pl.when(qi % (TK // TQ) == 1)
        def _():
            for hh in heads:
                consume(n_full, buf[0], buf[1], hh, "diag_long")

    buf_a, buf_b = (sa_ref, maxa_ref), (sb_ref, maxb_ref)
    m_ref[...] = jnp.full(m_ref.shape, NEG_BIG, f32)
    acc_ref[...] = jnp.zeros(acc_ref.shape, f32)
    for hh in heads:
        scores(0, sa_ref, maxa_ref, hh)

    def pair(jj, carry):
        half_step(2 * jj, buf_a, buf_b)
        half_step(2 * jj + 1, buf_b, buf_a)
        return carry

    lax.fori_loop(0, n_full // 2, pair, 0)

    @pl.when(n_full % 2 == 1)
    def _():
        half_step(n_full - 1, buf_a, buf_b)
        diagonal(buf_b)

    @pl.when(n_full % 2 == 0)
    def _():
        diagonal(buf_a)

    out_t = jnp.concatenate(
        [_rows8(acc_ref[hh][:MLA_V], lambda x, st: x / st, acc_ref[hh][MLA_V:MLA_V + SUBLANES])
         for hh in heads], axis=0)
    o_ref[...] = out_t.T.astype(bf16)


def _back_kernel(x_ref, ycs_ref, ot_ref, zm_ref, bgc_ref, bgm_ref, bgs_ref, wout_ref, *rest):
    f32, bf16 = jnp.float32, jnp.bfloat16
    ym = ot_ref[...].astype(f32) * _silu(zm_ref[...].astype(f32))
    ycs = ycs_ref[...].astype(f32)
    y = jnp.concatenate(
        [_rms(ycs[:, 0:CONV_WIDTH], bgc_ref[...], CONV_WIDTH).astype(bf16),
         _rms(ym, bgm_ref[...], MLA_WIDTH).astype(bf16),
         _rms(ycs[:, CONV_WIDTH:], bgs_ref[...], SG_WIDTH).astype(bf16)], axis=1)
    x_new = x_ref[...] + jnp.dot(y, wout_ref[...], preferred_element_type=f32)
    if len(rest) == 1:
        rest[0][...] = x_new
    else:
        next_g_ref, out_ref, next_h_ref = rest
        out_ref[...] = x_new
        next_h_ref[...] = _rms(x_new, next_g_ref[...], D_MODEL).astype(bf16)


def _rope_lane_tables(seq):
    half = MLA_ROPE // 2
    inv_freq = ROPE_THETA ** (-jnp.arange(half, dtype=jnp.float32) / half)
    ang = jnp.arange(seq, dtype=jnp.float32)[:, None] * inv_freq[None, :]
    cos, sin = jnp.cos(ang), jnp.sin(ang)
    ones = jnp.ones((seq, MLA_NOPE), jnp.float32)
    zpad = jnp.zeros((seq, HEAD_PAD - MLA_QK), jnp.float32)
    zn = jnp.zeros((seq, MLA_NOPE), jnp.float32)
    zh = jnp.zeros((seq, half), jnp.float32)
    c = jnp.concatenate([ones, cos, cos, zpad], axis=1)
    s_up = jnp.concatenate([zn, -sin, zh, zpad], axis=1)
    s_dn = jnp.concatenate([zn, zh, sin, zpad], axis=1)
    return c, s_up, s_dn


def _prep_params(p):
    bf16 = jnp.bfloat16
    depth = p['w_in'].shape[0]
    half = MLA_ROPE // 2
    w_in = p['w_in']
    kr0 = 3 * CONV_WIDTH + Q_LORA + KV_LORA
    win = jnp.zeros((depth, D_MODEL, IN_COLS_PAD), bf16)
    win = win.at[:, :, :OFF_KR].set(w_in[:, :, :kr0].astype(bf16))
    win = win.at[:, :, OFF_KR + MLA_NOPE:OFF_KR + MLA_QK].set(w_in[:, :, kr0:kr0 + MLA_ROPE].astype(bf16))
    win = win.at[:, :, OFF_ZM:].set(w_in[:, :, kr0 + MLA_ROPE:].astype(bf16))
    wq = p['w_uq'].astype(bf16).reshape(depth, Q_LORA, MLA_HEADS, MLA_QK)
    wuq = jnp.concatenate([wq, wq[..., MLA_NOPE + half:], wq[..., MLA_NOPE:MLA_NOPE + half]],
                          axis=3).reshape(depth, Q_LORA, QK_PAD)
    wukv = p['w_ukv'].astype(bf16).reshape(depth, KV_LORA, MLA_HEADS, MLA_NOPE + MLA_V)
    wk = jnp.pad(wukv[..., :MLA_NOPE], ((0, 0), (0, 0), (0, 0), (0, HEAD_PAD - MLA_NOPE))
                 ).reshape(depth, KV_LORA, QK_PAD)
    wvt = jnp.swapaxes(wukv[..., MLA_NOPE:].reshape(depth, KV_LORA, MLA_WIDTH), 1, 2)
    pad_g = lambda g: jnp.pad(g, ((0, 0), (0, HEAD_PAD - g.shape[1])))[:, None, :]
    gq = p['qk_q_g']
    partner = jnp.concatenate([jnp.zeros((depth, MLA_NOPE), gq.dtype), gq[:, MLA_NOPE + half:],
                               gq[:, MLA_NOPE:MLA_NOPE + half]], axis=1)
    bng = p['branch_norm_g']
    vec = lambda a: a[:, None, :]
    return dict(
        win=win, wuq=wuq, wk=wk, wvt=wvt,
        gq=jnp.concatenate([pad_g(gq), pad_g(partner)], axis=1), gk=pad_g(p['qk_k_g']),
        sgbm=jnp.repeat(jnp.swapaxes(p['sg_b'], 1, 2), SG_HEAD_DIM, axis=2),
        pww=p['conv_pw_w'].astype(bf16), wout=p['w_out'].astype(bf16),
        conv_w=p['conv_w'], sg_w=p['sg_w'],
        conv_b=vec(p['conv_b']), conv_ln_g=vec(p['conv_ln_g']), conv_ln_b=vec(p['conv_ln_b']),
        conv_pw_b=vec(p['conv_pw_b']), q_norm_g=vec(p['q_norm_g']), kv_norm_g=vec(p['kv_norm_g']),
        sg_ln_g=vec(p['sg_ln_g']), sg_ln_b=vec(p['sg_ln_b']), norm_g=vec(p['norm_g']),
        bg_conv=vec(bng[:, :CONV_WIDTH]), bg_mla=vec(bng[:, CONV_WIDTH:CONV_WIDTH + MLA_WIDTH]),
        bg_sg=vec(bng[:, CONV_WIDTH + MLA_WIDTH:]))


def _layer(x, h, rope, w, layer, emit_next_h):
    B, S, _ = x.shape
    n_tok = B * S
    tstep = FRONT_TILES * TM
    n_steps = n_tok // tstep
    steps_per_batch = S // tstep
    bf16 = jnp.bfloat16

    def of_layer(name, which=layer):
        shape = w[name].shape[1:]
        return w[name], pl.BlockSpec((None,) + shape, lambda *_: (which,) + (0,) * len(shape),
                                     pipeline_mode=pl.Buffered(1))

    ropespec = pl.BlockSpec((tstep, HEAD_PAD), lambda s: (s % steps_per_batch, 0))
    front_in = [
        (h, pl.BlockSpec((tstep, D_MODEL), lambda s: (s, 0))),
        (h, pl.BlockSpec((TM, D_MODEL),
                         lambda s: (jnp.minimum(FRONT_TILES * (s + 1), n_tok // TM - 1), 0))),
        of_layer('norm_g'), of_layer('win'), of_layer('conv_w'), of_layer('conv_b'), of_layer('conv_ln_g'),
        of_layer('conv_ln_b'), of_layer('pww'), of_layer('conv_pw_b'), of_layer('q_norm_g'),
        of_layer('wuq'), of_layer('kv_norm_g'), of_layer('wk'), of_layer('wvt'),
        of_layer('gq'), of_layer('gk'), of_layer('sg_ln_g'), of_layer('sg_ln_b'),
        of_layer('sg_w'), of_layer('sgbm'),
        (rope[0], ropespec), (rope[1], ropespec), (rope[2], ropespec),
    ]
    otile = lambda width: pl.BlockSpec((tstep, width), lambda s: (s, 0))
    ycs, q, k, vt, zm = pl.pallas_call(
        functools.partial(_front_kernel, steps_per_batch=steps_per_batch,
                          normalise_input=h.dtype != bf16),
        grid=(n_steps,),
        in_specs=[s for _, s in front_in],
        out_specs=[otile(CONV_WIDTH + SG_WIDTH), otile(QK_PAD), otile(QK_PAD),
                   pl.BlockSpec((tstep // TK, MLA_WIDTH, TK), lambda s: (s, 0, 0)),
                   otile(MLA_WIDTH)],
        out_shape=[jax.ShapeDtypeStruct((n_tok, CONV_WIDTH + SG_WIDTH), bf16),
                   jax.ShapeDtypeStruct((n_tok, QK_PAD), bf16),
                   jax.ShapeDtypeStruct((n_tok, QK_PAD), bf16),
                   jax.ShapeDtypeStruct((n_tok // TK, MLA_WIDTH, TK), bf16),
                   jax.ShapeDtypeStruct((n_tok, MLA_WIDTH), bf16)],
        scratch_shapes=[pltpu.VMEM((TM, IN_COLS_PAD), jnp.float32),
                        pltpu.VMEM((CONV_HALO + TM, CONV_WIDTH), jnp.float32),
                        pltpu.VMEM((SUBLANES - 1, CONV_HALO + TM, CONV_WIDTH), jnp.float32)],
        compiler_params=pltpu.CompilerParams(
            dimension_semantics=("arbitrary",), vmem_limit_bytes=VMEM_LIMIT),
        name="front",
    )(*[a for a, _ in front_in])

    nq, nkv = S // TQ, S // TK
    ot = pl.pallas_call(
        _attn_kernel,
        grid=(B, nq),
        in_specs=[pl.BlockSpec((None, TQ, QK_PAD), lambda b, i: (b, i, 0)),
                  pl.BlockSpec((None, S, QK_PAD), lambda b, i: (b, 0, 0),
                               pipeline_mode=pl.Buffered(1)),
                  pl.BlockSpec((None, nkv, MLA_WIDTH, TK), lambda b, i: (b, 0, 0, 0),
                               pipeline_mode=pl.Buffered(1))],
        out_specs=pl.BlockSpec((None, TQ, MLA_WIDTH), lambda b, i: (b, i, 0)),
        out_shape=jax.ShapeDtypeStruct((B, S, MLA_WIDTH), bf16),
        scratch_shapes=[pltpu.VMEM((MLA_HEADS, TK, TQ), jnp.float32)] * 2
                       + [pltpu.VMEM((MLA_HEADS, SUBLANES, TQ), jnp.float32)] * 3
                       + [pltpu.VMEM((MLA_HEADS, ACC_ROWS, TQ), jnp.float32)],
        compiler_params=pltpu.CompilerParams(
            dimension_semantics=("arbitrary", "arbitrary"), vmem_limit_bytes=VMEM_LIMIT),
        name="attn",
    )(q.reshape(B, S, QK_PAD), k.reshape(B, S, QK_PAD), vt.reshape(B, nkv, MLA_WIDTH, TK))

    tile = lambda width: pl.BlockSpec((None, TB, width), lambda b, i: (b, i, 0))
    back_in = [(x, tile(D_MODEL)), (ycs.reshape(B, S, -1), tile(CONV_WIDTH + SG_WIDTH)),
               (ot, tile(MLA_WIDTH)), (zm.reshape(B, S, -1), tile(MLA_WIDTH)),
               of_layer('bg_conv'), of_layer('bg_mla'), of_layer('bg_sg'), of_layer('wout')]
    out_specs = [tile(D_MODEL)]
    out_shape = [jax.ShapeDtypeStruct((B, S, D_MODEL), jnp.float32)]
    if emit_next_h:
        back_in.append(of_layer('norm_g', layer + 1))
        out_specs.append(tile(D_MODEL))
        out_shape.append(jax.ShapeDtypeStruct((B, S, D_MODEL), bf16))
    outs = pl.pallas_call(
        _back_kernel,
        grid=(B, S // TB),
        in_specs=[s for _, s in back_in],
        out_specs=out_specs,
        out_shape=out_shape,
        compiler_params=pltpu.CompilerParams(
            dimension_semantics=("arbitrary", "arbitrary"), vmem_limit_bytes=VMEM_LIMIT),
        name="back",
    )(*[a for a, _ in back_in])
    return outs[0], (outs[1].reshape(n_tok, D_MODEL) if emit_next_h else None)


def kernel(x, norm_g, w_in, conv_w, conv_b, conv_ln_g, conv_ln_b, conv_pw_w, conv_pw_b,
           q_norm_g, w_uq, kv_norm_g, w_ukv, qk_q_g, qk_k_g, sg_ln_g, sg_ln_b, sg_w, sg_b,
           branch_norm_g, w_out):
    params = dict(norm_g=norm_g, w_in=w_in, conv_w=conv_w, conv_b=conv_b, conv_ln_g=conv_ln_g,
                  conv_ln_b=conv_ln_b, conv_pw_w=conv_pw_w, conv_pw_b=conv_pw_b,
                  q_norm_g=q_norm_g, w_uq=w_uq, kv_norm_g=kv_norm_g, w_ukv=w_ukv,
                  qk_q_g=qk_q_g, qk_k_g=qk_k_g, sg_ln_g=sg_ln_g, sg_ln_b=sg_ln_b, sg_w=sg_w,
                  sg_b=sg_b, branch_norm_g=branch_norm_g, w_out=w_out)
    rope = _rope_lane_tables(x.shape[1])
    depth = norm_g.shape[0]
    w = _prep_params(params)
    h = x.reshape(-1, D_MODEL)
    for layer in range(depth):
        x, h = _layer(x, h, rope, w, layer, layer + 1 < depth)
    return x
```

```python
import functools

import jax
import jax.numpy as jnp
from jax import lax
from jax.experimental import pallas as pl
from jax.experimental.pallas import tpu as pltpu

D_MODEL = 1024
CONV_WIDTH = 256
CONV_K = 31
MLA_HEADS = 8
MLA_NOPE = 64
MLA_ROPE = 32
MLA_QK = MLA_NOPE + MLA_ROPE
MLA_V = 64
MLA_WIDTH = MLA_HEADS * MLA_V
Q_LORA = 768
KV_LORA = 256
ROPE_THETA = 10000.0
SG_WIDTH = 256
SG_HEADS = 4
SG_HEAD_DIM = SG_WIDTH // SG_HEADS
SG_CHUNK = 128
EPS = 1e-6

SUBLANES = 8
LANES = 128
HEAD_PAD = LANES
QK_PAD = MLA_HEADS * HEAD_PAD

OFF_A = 0
OFF_GLU = OFF_A + CONV_WIDTH
OFF_ZC = OFF_GLU + CONV_WIDTH
OFF_CQ = OFF_ZC + CONV_WIDTH
OFF_CKV = OFF_CQ + Q_LORA
OFF_KR = OFF_CKV + KV_LORA
OFF_ZM = OFF_KR + HEAD_PAD
OFF_U = OFF_ZM + MLA_WIDTH
OFF_V = OFF_U + SG_WIDTH
OFF_ZS = OFF_V + SG_WIDTH
IN_COLS_PAD = OFF_ZS + SG_WIDTH

CONV_HALO = 32
CONV_ROWS = 64
TM = 256
TB = 512
FRONT_TILES = 4
TQ = 256
TK = 512
ACC_ROWS = MLA_V + 16
NEG_BIG = -1e30
LOG2E = 1.4426950408889634
VMEM_LIMIT = 48 * 1024 * 1024

assert FRONT_TILES >= 2 and (FRONT_TILES * TM) % TK == 0 and TK % TM == 0 and TM % SG_CHUNK == 0 and TK == 2 * TQ


def _rms(x, g, width):
    ss = jnp.sum(x * x, axis=-1, keepdims=True)
    return x * lax.rsqrt(ss + width * EPS) * (g * width ** 0.5)


def _layer_norm(x, g, b, width):
    mu = jnp.sum(x, axis=-1, keepdims=True) * (1.0 / width)
    xc = x - mu
    ss = jnp.sum(xc * xc, axis=-1, keepdims=True)
    return xc * lax.rsqrt(ss + width * EPS) * (g * width ** 0.5) + b


def _sigmoid(x):
    return 0.5 * jnp.tanh(0.5 * x) + 0.5


def _silu(x):
    return x * _sigmoid(x)


def _rope(x, c, s_up, s_dn):
    return x * c + pltpu.roll(x, LANES - MLA_ROPE // 2, 1) * s_up + pltpu.roll(x, MLA_ROPE // 2, 1) * s_dn


def _front_kernel(h_ref, hnext_ref, ing_ref, win_ref, convw_ref, convb_ref, clng_ref,
                  clnb_ref, pww_ref, pwb_ref, qng_ref, wuq_ref, kvng_ref, wk_ref, wvt_ref,
                  gq_ref, gk_ref, sglng_ref, sglnb_ref, sgw_ref, sgbm_ref,
                  rc_ref, rsu_ref, rsd_ref,
                  ycs_ref, q_ref, k_ref, vt_ref, zm_ref,
                  proj0_ref, ybuf_ref, zs_ref, *, steps_per_batch, normalise_input):
    f32, bf16 = jnp.float32, jnp.bfloat16
    step = pl.program_id(0)

    def project(h_tile):
        if normalise_input:
            h_tile = _rms(h_tile, ing_ref[...], D_MODEL).astype(bf16)
        return jnp.dot(h_tile, win_ref[...], preferred_element_type=f32)

    tile_rows = lambda t: h_ref[t * TM:(t + 1) * TM, :]

    def up_project(proj, t):
        cq = _rms(proj[:, OFF_CQ:OFF_CQ + Q_LORA], qng_ref[...], Q_LORA).astype(bf16)
        qf = jnp.dot(cq, wuq_ref[...], preferred_element_type=f32)
        ckv = _rms(proj[:, OFF_CKV:OFF_CKV + KV_LORA], kvng_ref[...], KV_LORA).astype(bf16)
        kf = jnp.dot(ckv, wk_ref[...], preferred_element_type=f32)
        vt = lax.dot_general(wvt_ref[...], ckv, (((1,), (1,)), ((), ())),
                             preferred_element_type=f32)
        vt_ref[t * TM // TK, :, (t * TM) % TK:(t * TM) % TK + TM] = vt.astype(bf16)
        return qf, kf

    def mix(proj, qf, kf, t):
        rows = slice(t * TM, (t + 1) * TM)
        col = lambda off, w: proj[:, off:off + w]

        v = _layer_norm(jax.nn.gelu(col(OFF_V, SG_WIDTH)), sglng_ref[...], sglnb_ref[...], SG_WIDTH)
        trow = lax.broadcasted_iota(jnp.int32, (SG_CHUNK, SG_CHUNK), 0)
        tcol = lax.broadcasted_iota(jnp.int32, (SG_CHUNK, SG_CHUNK), 1)
        wcat = jnp.concatenate(
            [jnp.where(tcol <= trow, sgw_ref[g], 0.0) for g in range(SG_HEADS)], axis=1).astype(bf16)
        lane = lax.broadcasted_iota(jnp.int32, (SG_CHUNK, SG_WIDTH), 1)
        mixed = []
        for c in range(TM // SG_CHUNK):
            vc = v[c * SG_CHUNK:(c + 1) * SG_CHUNK, :]
            vstack = jnp.concatenate(
                [jnp.where((lane >= g * SG_HEAD_DIM) & (lane < (g + 1) * SG_HEAD_DIM), vc, 0.0)
                 for g in range(SG_HEADS)], axis=0).astype(bf16)
            mixed.append(jnp.dot(wcat, vstack, preferred_element_type=f32) + sgbm_ref[...])
        ys = jax.nn.gelu(col(OFF_U, SG_WIDTH)) * jnp.concatenate(mixed, axis=0) * _silu(col(OFF_ZS, SG_WIDTH))
        ycs_ref[rows, CONV_WIDTH:CONV_WIDTH + SG_WIDTH] = ys.astype(bf16)

        ybuf_ref[CONV_HALO:CONV_HALO + TM, :] = col(OFF_A, CONV_WIDTH) * _sigmoid(col(OFF_GLU, CONV_WIDTH))
        first = CONV_HALO - (CONV_K - 1)
        for r in range(1, SUBLANES):
            zs_ref[r - 1, 0:TM + CONV_HALO - SUBLANES, :] = ybuf_ref[r:r + TM + CONV_HALO - SUBLANES, :]
        chunks = []
        for c0 in range(0, TM, CONV_ROWS):
            conv = jnp.zeros((CONV_ROWS, CONV_WIDTH), f32) + convb_ref[...]
            for tap in range(CONV_K):
                r, off = (tap + first) % SUBLANES, (tap + first) // SUBLANES * SUBLANES + c0
                src = ybuf_ref if r == 0 else zs_ref.at[r - 1]
                conv = conv + src[off:off + CONV_ROWS, :] * convw_ref[tap:tap + 1, :]
            chunks.append(_silu(_layer_norm(conv, clng_ref[...], clnb_ref[...], CONV_WIDTH)).astype(bf16))
        ybuf_ref[0:CONV_HALO, :] = ybuf_ref[TM:TM + CONV_HALO, :]
        yc = jnp.concatenate(chunks, axis=0)
        yc = jnp.dot(yc, pww_ref[...], preferred_element_type=f32) + pwb_ref[...]
        yc = yc * _silu(col(OFF_ZC, CONV_WIDTH))
        ycs_ref[rows, 0:CONV_WIDTH] = yc.astype(bf16)

        zm_ref[rows, :] = col(OFF_ZM, MLA_WIDTH).astype(bf16)
        rc, rsu, rsd = rc_ref[rows, :], rsu_ref[rows, :], rsd_ref[rows, :]
        qc = rc * (gq_ref[0:1, :] * LOG2E)
        qs = (rsu + rsd) * (gq_ref[1:2, :] * LOG2E)
        real = (lax.broadcasted_iota(jnp.int32, (1, HEAD_PAD), 1) < MLA_QK).astype(f32)
        for hd in range(MLA_HEADS):
            qh = qf[:, hd * HEAD_PAD:(hd + 1) * HEAD_PAD]
            ss = jnp.sum(qh * qh * real, axis=-1, keepdims=True)
            roped = qh * qc + pltpu.roll(qh, LANES - MLA_ROPE, 1) * qs
            q_ref[rows, hd * HEAD_PAD:(hd + 1) * HEAD_PAD] = (
                roped * lax.rsqrt(ss + MLA_QK * EPS)).astype(bf16)
        gk = gk_ref[...] * MLA_QK ** 0.5
        kr = col(OFF_KR, HEAD_PAD)
        ss_r = jnp.sum(kr * kr, axis=-1, keepdims=True)
        krr = _rope(kr * gk, rc, rsu, rsd)
        for hd in range(MLA_HEADS):
            kh = kf[:, hd * HEAD_PAD:(hd + 1) * HEAD_PAD]
            ss = jnp.sum(kh * kh, axis=-1, keepdims=True) + ss_r
            k_ref[rows, hd * HEAD_PAD:(hd + 1) * HEAD_PAD] = (
                (kh * gk + krr) * lax.rsqrt(ss + MLA_QK * EPS)).astype(bf16)

    @pl.when(step % steps_per_batch == 0)
    def _():
        ybuf_ref[0:CONV_HALO, :] = jnp.zeros((CONV_HALO, CONV_WIDTH), f32)

    @pl.when(step == 0)
    def _():
        proj0_ref[...] = project(tile_rows(0))

    projs = {0: proj0_ref, 1: project(tile_rows(1))}
    for t in range(FRONT_TILES):
        proj = projs.pop(t)
        qf, kf = up_project(proj, t)
        if t + 2 < FRONT_TILES:
            projs[t + 2] = project(tile_rows(t + 2))
        mix(proj, qf, kf, t)
    proj0_ref[...] = project(hnext_ref[...])


def _col_max8(s):
    m8 = jnp.max(s.reshape(s.shape[0] // SUBLANES, SUBLANES, s.shape[1]), axis=0)
    for shift in (4, 2, 1):
        m8 = jnp.maximum(m8, pltpu.roll(m8, shift, 0))
    return m8


def _rows8(x, fn, stat8):
    x3 = x.reshape(x.shape[0] // SUBLANES, SUBLANES, x.shape[1])
    return fn(x3, stat8[None]).reshape(x.shape)


def _attn_kernel(q_ref, k_ref, vt_ref, o_ref, sa_ref, sb_ref, maxa_ref, maxb_ref, m_ref, acc_ref):
    f32, bf16 = jnp.float32, jnp.bfloat16
    qi = pl.program_id(1)
    n_full = qi // (TK // TQ)
    causal = (lax.broadcasted_iota(jnp.int32, (TQ, TQ), 0)
              <= lax.broadcasted_iota(jnp.int32, (TQ, TQ), 1))
    ones_rows = jnp.ones((ACC_ROWS - MLA_V, TK), bf16)
    heads = range(MLA_HEADS)
    hslab = lambda hh: slice(hh * HEAD_PAD, (hh + 1) * HEAD_PAD)
    vslab = lambda hh: slice(hh * MLA_V, (hh + 1) * MLA_V)

    def scores(j, s_ref, max_ref, hh):
        kb = k_ref[pl.ds(pl.multiple_of(j * TK, TK), TK), hslab(hh)]
        s = lax.dot_general(kb, q_ref[:, hslab(hh)], (((1,), (1,)), ((), ())),
                            preferred_element_type=f32)
        s_ref[hh] = s
        max_ref[hh] = _col_max8(s)

    def consume(j, s_ref, max_ref, hh, mode):
        if mode == "full":
            s, blk_max = s_ref[hh], max_ref[hh]
        elif mode == "diag_short":
            s = jnp.where(causal, s_ref[hh, 0:TQ, :], NEG_BIG)
            blk_max = _col_max8(s)
        else:
            s = jnp.concatenate([s_ref[hh, 0:TQ, :],
                                 jnp.where(causal, s_ref[hh, TQ:TK, :], NEG_BIG)], axis=0)
            blk_max = _col_max8(s)
        keys = s.shape[0]
        m = m_ref[hh]
        m_new = jnp.maximum(m, blk_max)
        alpha = jnp.exp2(m - m_new)
        p = _rows8(s, lambda x, st: jnp.exp2(x - st), m_new)
        m_ref[hh] = m_new
        lhs = jnp.concatenate([vt_ref[j, vslab(hh), 0:keys], ones_rows[:, 0:keys]], axis=0)
        pv = jnp.dot(lhs, p.astype(bf16), preferred_element_type=f32)
        acc_ref[hh] = _rows8(acc_ref[hh], lambda x, st: x * st, alpha) + pv

    def half_step(j, cur, nxt):
        for hh in heads:
            scores(j + 1, nxt[0], nxt[1], hh)
            consume(j, cur[0], cur[1], hh, "full")

    def diagonal(buf):
        @pl.when(qi % (TK // TQ) == 0)
        def _():
            for hh in heads:
                consume(n_full, buf[0], buf[1], hh, "diag_short")

        @pl.when(qi % (TK // TQ) == 1)
        def _():
            for hh in heads:
                consume(n_full, buf[0], buf[1], hh, "diag_long")

    buf_a, buf_b = (sa_ref, maxa_ref), (sb_ref, maxb_ref)
    m_ref[...] = jnp.full(m_ref.shape, NEG_BIG, f32)
    acc_ref[...] = jnp.zeros(acc_ref.shape, f32)
    for hh in heads:
        scores(0, sa_ref, maxa_ref, hh)

    def pair(j):
        half_step(j, buf_a, buf_b)
        half_step(j + 1, buf_b, buf_a)

    def two_pairs(jj, carry):
        pair(4 * jj)
        pair(4 * jj + 2)
        return carry

    lax.fori_loop(0, n_full // 4, two_pairs, 0)
    left = n_full % 4

    @pl.when(left >= 2)
    def _():
        pair(n_full - left)

    @pl.when(left % 2 == 1)
    def _():
        half_step(n_full - 1, buf_a, buf_b)
        diagonal(buf_b)

    @pl.when(left % 2 == 0)
    def _():
        diagonal(buf_a)

    out_t = jnp.concatenate(
        [_rows8(acc_ref[hh][:MLA_V], lambda x, st: x / st, acc_ref[hh][MLA_V:MLA_V + SUBLANES])
         for hh in heads], axis=0)
    o_ref[...] = out_t.T.astype(bf16)


def _back_kernel(x_ref, ycs_ref, ot_ref, zm_ref, bgc_ref, bgm_ref, bgs_ref, wout_ref, *rest):
    f32, bf16 = jnp.float32, jnp.bfloat16
    ym = ot_ref[...].astype(f32) * _silu(zm_ref[...].astype(f32))
    ycs = ycs_ref[...].astype(f32)
    y = jnp.concatenate(
        [_rms(ycs[:, 0:CONV_WIDTH], bgc_ref[...], CONV_WIDTH).astype(bf16),
         _rms(ym, bgm_ref[...], MLA_WIDTH).astype(bf16),
         _rms(ycs[:, CONV_WIDTH:], bgs_ref[...], SG_WIDTH).astype(bf16)], axis=1)
    x_new = x_ref[...] + jnp.dot(y, wout_ref[...], preferred_element_type=f32)
    if len(rest) == 1:
        rest[0][...] = x_new
    else:
        next_g_ref, out_ref, next_h_ref = rest
        out_ref[...] = x_new
        next_h_ref[...] = _rms(x_new, next_g_ref[...], D_MODEL).astype(bf16)


def _rope_lane_tables(seq):
    half = MLA_ROPE // 2
    inv_freq = ROPE_THETA ** (-jnp.arange(half, dtype=jnp.float32) / half)
    per_row = LANES // half
    pos = (jnp.arange(seq // per_row, dtype=jnp.int32)[:, None] * per_row
           + jnp.arange(LANES, dtype=jnp.int32)[None, :] // half).astype(jnp.float32)
    ang = pos * jnp.tile(inv_freq, per_row)[None, :]
    cos, sin = jnp.cos(ang).reshape(seq, half), jnp.sin(ang).reshape(seq, half)
    ones = jnp.ones((seq, MLA_NOPE), jnp.float32)
    zpad = jnp.zeros((seq, HEAD_PAD - MLA_QK), jnp.float32)
    zn = jnp.zeros((seq, MLA_NOPE), jnp.float32)
    zh = jnp.zeros((seq, half), jnp.float32)
    c = jnp.concatenate([ones, cos, cos, zpad], axis=1)
    s_up = jnp.concatenate([zn, -sin, zh, zpad], axis=1)
    s_dn = jnp.concatenate([zn, zh, sin, zpad], axis=1)
    return c, s_up, s_dn


def _prep_params(p):
    bf16 = jnp.bfloat16
    depth = p['w_in'].shape[0]
    half = MLA_ROPE // 2
    w_in = p['w_in']
    kr0 = 3 * CONV_WIDTH + Q_LORA + KV_LORA
    win = jnp.zeros((depth, D_MODEL, IN_COLS_PAD), bf16)
    win = win.at[:, :, :OFF_KR].set(w_in[:, :, :kr0].astype(bf16))
    win = win.at[:, :, OFF_KR + MLA_NOPE:OFF_KR + MLA_QK].set(w_in[:, :, kr0:kr0 + MLA_ROPE].astype(bf16))
    win = win.at[:, :, OFF_ZM:].set(w_in[:, :, kr0 + MLA_ROPE:].astype(bf16))
    wq = p['w_uq'].astype(bf16).reshape(depth, Q_LORA, MLA_HEADS, MLA_QK)
    wuq = jnp.concatenate([wq, wq[..., MLA_NOPE + half:], wq[..., MLA_NOPE:MLA_NOPE + half]],
                          axis=3).reshape(depth, Q_LORA, QK_PAD)
    wukv = p['w_ukv'].astype(bf16).reshape(depth, KV_LORA, MLA_HEADS, MLA_NOPE + MLA_V)
    wk = jnp.pad(wukv[..., :MLA_NOPE], ((0, 0), (0, 0), (0, 0), (0, HEAD_PAD - MLA_NOPE))
                 ).reshape(depth, KV_LORA, QK_PAD)
    wvt = jnp.swapaxes(wukv[..., MLA_NOPE:].reshape(depth, KV_LORA, MLA_WIDTH), 1, 2)
    pad_g = lambda g: jnp.pad(g, ((0, 0), (0, HEAD_PAD - g.shape[1])))[:, None, :]
    gq = p['qk_q_g']
    partner = jnp.concatenate([jnp.zeros((depth, MLA_NOPE), gq.dtype), gq[:, MLA_NOPE + half:],
                               gq[:, MLA_NOPE:MLA_NOPE + half]], axis=1)
    bng = p['branch_norm_g']
    vec = lambda a: a[:, None, :]
    return dict(
        win=win, wuq=wuq, wk=wk, wvt=wvt,
        gq=jnp.concatenate([pad_g(gq), pad_g(partner)], axis=1), gk=pad_g(p['qk_k_g']),
        sgbm=jnp.repeat(jnp.swapaxes(p['sg_b'], 1, 2), SG_HEAD_DIM, axis=2),
        pww=p['conv_pw_w'].astype(bf16), wout=p['w_out'].astype(bf16),
        conv_w=p['conv_w'], sg_w=p['sg_w'],
        conv_b=vec(p['conv_b']), conv_ln_g=vec(p['conv_ln_g']), conv_ln_b=vec(p['conv_ln_b']),
        conv_pw_b=vec(p['conv_pw_b']), q_norm_g=vec(p['q_norm_g']), kv_norm_g=vec(p['kv_norm_g']),
        sg_ln_g=vec(p['sg_ln_g']), sg_ln_b=vec(p['sg_ln_b']), norm_g=vec(p['norm_g']),
        bg_conv=vec(bng[:, :CONV_WIDTH]), bg_mla=vec(bng[:, CONV_WIDTH:CONV_WIDTH + MLA_WIDTH]),
        bg_sg=vec(bng[:, CONV_WIDTH + MLA_WIDTH:]))


def _layer(x, h, rope, w, layer, emit_next_h):
    B, S, _ = x.shape
    n_tok = B * S
    tstep = FRONT_TILES * TM
    n_steps = n_tok // tstep
    steps_per_batch = S // tstep
    bf16 = jnp.bfloat16

    def of_layer(name, which=layer):
        shape = w[name].shape[1:]
        return w[name], pl.BlockSpec((None,) + shape, lambda *_: (which,) + (0,) * len(shape),
                                     pipeline_mode=pl.Buffered(1))

    ropespec = pl.BlockSpec((tstep, HEAD_PAD), lambda s: (s % steps_per_batch, 0))
    front_in = [
        (h, pl.BlockSpec((tstep, D_MODEL), lambda s: (s, 0))),
        (h, pl.BlockSpec((TM, D_MODEL),
                         lambda s: (jnp.minimum(FRONT_TILES * (s + 1), n_tok // TM - 1), 0))),
        of_layer('norm_g'), of_layer('win'), of_layer('conv_w'), of_layer('conv_b'), of_layer('conv_ln_g'),
        of_layer('conv_ln_b'), of_layer('pww'), of_layer('conv_pw_b'), of_layer('q_norm_g'),
        of_layer('wuq'), of_layer('kv_norm_g'), of_layer('wk'), of_layer('wvt'),
        of_layer('gq'), of_layer('gk'), of_layer('sg_ln_g'), of_layer('sg_ln_b'),
        of_layer('sg_w'), of_layer('sgbm'),
        (rope[0], ropespec), (rope[1], ropespec), (rope[2], ropespec),
    ]
    otile = lambda width: pl.BlockSpec((tstep, width), lambda s: (s, 0))
    ycs, q, k, vt, zm = pl.pallas_call(
        functools.partial(_front_kernel, steps_per_batch=steps_per_batch,
                          normalise_input=h.dtype != bf16),
        grid=(n_steps,),
        in_specs=[s for _, s in front_in],
        out_specs=[otile(CONV_WIDTH + SG_WIDTH), otile(QK_PAD), otile(QK_PAD),
                   pl.BlockSpec((tstep // TK, MLA_WIDTH, TK), lambda s: (s, 0, 0)),
                   otile(MLA_WIDTH)],
        out_shape=[jax.ShapeDtypeStruct((n_tok, CONV_WIDTH + SG_WIDTH), bf16),
                   jax.ShapeDtypeStruct((n_tok, QK_PAD), bf16),
                   jax.ShapeDtypeStruct((n_tok, QK_PAD), bf16),
                   jax.ShapeDtypeStruct((n_tok // TK, MLA_WIDTH, TK), bf16),
                   jax.ShapeDtypeStruct((n_tok, MLA_WIDTH), bf16)],
        scratch_shapes=[pltpu.VMEM((TM, IN_COLS_PAD), jnp.float32),
                        pltpu.VMEM((CONV_HALO + TM, CONV_WIDTH), jnp.float32),
                        pltpu.VMEM((SUBLANES - 1, CONV_HALO + TM, CONV_WIDTH), jnp.float32)],
        compiler_params=pltpu.CompilerParams(
            dimension_semantics=("arbitrary",), vmem_limit_bytes=VMEM_LIMIT),
        name="front",
    )(*[a for a, _ in front_in])

    nq, nkv = S // TQ, S // TK
    ot = pl.pallas_call(
        _attn_kernel,
        grid=(B, nq),
        in_specs=[pl.BlockSpec((None, TQ, QK_PAD), lambda b, i: (b, i, 0)),
                  pl.BlockSpec((None, S, QK_PAD), lambda b, i: (b, 0, 0),
                               pipeline_mode=pl.Buffered(1)),
                  pl.BlockSpec((None, nkv, MLA_WIDTH, TK), lambda b, i: (b, 0, 0, 0),
                               pipeline_mode=pl.Buffered(1))],
        out_specs=pl.BlockSpec((None, TQ, MLA_WIDTH), lambda b, i: (b, i, 0)),
        out_shape=jax.ShapeDtypeStruct((B, S, MLA_WIDTH), bf16),
        scratch_shapes=[pltpu.VMEM((MLA_HEADS, TK, TQ), jnp.float32)] * 2
                       + [pltpu.VMEM((MLA_HEADS, SUBLANES, TQ), jnp.float32)] * 3
                       + [pltpu.VMEM((MLA_HEADS, ACC_ROWS, TQ), jnp.float32)],
        compiler_params=pltpu.CompilerParams(
            dimension_semantics=("arbitrary", "arbitrary"), vmem_limit_bytes=VMEM_LIMIT),
        name="attn",
    )(q.reshape(B, S, QK_PAD), k.reshape(B, S, QK_PAD), vt.reshape(B, nkv, MLA_WIDTH, TK))

    tile = lambda width: pl.BlockSpec((None, TB, width), lambda b, i: (b, i, 0))
    back_in = [(x, tile(D_MODEL)), (ycs.reshape(B, S, -1), tile(CONV_WIDTH + SG_WIDTH)),
               (ot, tile(MLA_WIDTH)), (zm.reshape(B, S, -1), tile(MLA_WIDTH)),
               of_layer('bg_conv'), of_layer('bg_mla'), of_layer('bg_sg'), of_layer('wout')]
    out_specs = [tile(D_MODEL)]
    out_shape = [jax.ShapeDtypeStruct((B, S, D_MODEL), jnp.float32)]
    if emit_next_h:
        back_in.append(of_layer('norm_g', layer + 1))
        out_specs.append(tile(D_MODEL))
        out_shape.append(jax.ShapeDtypeStruct((B, S, D_MODEL), bf16))
    outs = pl.pallas_call(
        _back_kernel,
        grid=(B, S // TB),
        in_specs=[s for _, s in back_in],
        out_specs=out_specs,
        out_shape=out_shape,
        compiler_params=pltpu.CompilerParams(
            dimension_semantics=("arbitrary", "arbitrary"), vmem_limit_bytes=VMEM_LIMIT),
        name="back",
    )(*[a for a, _ in back_in])
    return outs[0], (outs[1].reshape(n_tok, D_MODEL) if emit_next_h else None)


def kernel(x, norm_g, w_in, conv_w, conv_b, conv_ln_g, conv_ln_b, conv_pw_w, conv_pw_b,
           q_norm_g, w_uq, kv_norm_g, w_ukv, qk_q_g, qk_k_g, sg_ln_g, sg_ln_b, sg_w, sg_b,
           branch_norm_g, w_out):
    params = dict(norm_g=norm_g, w_in=w_in, conv_w=conv_w, conv_b=conv_b, conv_ln_g=conv_ln_g,
                  conv_ln_b=conv_ln_b, conv_pw_w=conv_pw_w, conv_pw_b=conv_pw_b,
                  q_norm_g=q_norm_g, w_uq=w_uq, kv_norm_g=kv_norm_g, w_ukv=w_ukv,
                  qk_q_g=qk_q_g, qk_k_g=qk_k_g, sg_ln_g=sg_ln_g, sg_ln_b=sg_ln_b, sg_w=sg_w,
                  sg_b=sg_b, branch_norm_g=branch_norm_g, w_out=w_out)
    rope = _rope_lane_tables(x.shape[1])
    depth = norm_g.shape[0]
    w = _prep_params(params)
    h = x.reshape(-1, D_MODEL)
    for layer in range(depth):
        x, h = _layer(x, h, rope, w, layer, layer + 1 < depth)
    return x
```

```python
import functools

import jax
import jax.numpy as jnp
from jax import lax
from jax.experimental import pallas as pl
from jax.experimental.pallas import tpu as pltpu

D_MODEL = 1024
CONV_WIDTH = 256
CONV_K = 31
MLA_HEADS = 8
MLA_NOPE = 64
MLA_ROPE = 32
MLA_QK = MLA_NOPE + MLA_ROPE
MLA_V = 64
MLA_WIDTH = MLA_HEADS * MLA_V
Q_LORA = 768
KV_LORA = 256
ROPE_THETA = 10000.0
SG_WIDTH = 256
SG_HEADS = 4
SG_HEAD_DIM = SG_WIDTH // SG_HEADS
SG_CHUNK = 128
EPS = 1e-6

SUBLANES = 8
LANES = 128
HEAD_PAD = LANES
QK_PAD = MLA_HEADS * HEAD_PAD

OFF_A = 0
OFF_GLU = OFF_A + CONV_WIDTH
OFF_ZC = OFF_GLU + CONV_WIDTH
OFF_CQ = OFF_ZC + CONV_WIDTH
OFF_CKV = OFF_CQ + Q_LORA
OFF_KR = OFF_CKV + KV_LORA
OFF_ZM = OFF_KR + HEAD_PAD
OFF_U = OFF_ZM + MLA_WIDTH
OFF_V = OFF_U + SG_WIDTH
OFF_ZS = OFF_V + SG_WIDTH
IN_COLS_PAD = OFF_ZS + SG_WIDTH

CONV_HALO = 32
CONV_ROWS = 64
TM = 256
TB = 512
FRONT_TILES = 4
TQ = 256
TK = 512
ACC_ROWS = MLA_V + 16
NEG_BIG = -1e30
LOG2E = 1.4426950408889634
VMEM_LIMIT = 48 * 1024 * 1024

assert FRONT_TILES >= 2 and (FRONT_TILES * TM) % TK == 0 and TK % TM == 0 and TM % SG_CHUNK == 0 and TK == 2 * TQ


def _rms(x, g, width):
    ss = jnp.sum(x * x, axis=-1, keepdims=True)
    return x * lax.rsqrt(ss + width * EPS) * (g * width ** 0.5)


def _layer_norm(x, g, b, width):
    mu = jnp.sum(x, axis=-1, keepdims=True) * (1.0 / width)
    xc = x - mu
    ss = jnp.sum(xc * xc, axis=-1, keepdims=True)
    return xc * lax.rsqrt(ss + width * EPS) * (g * width ** 0.5) + b


def _sigmoid(x):
    return 0.5 * jnp.tanh(0.5 * x) + 0.5


def _silu(x):
    return x * _sigmoid(x)


def _rope(x, c, s_up, s_dn):
    return x * c + pltpu.roll(x, LANES - MLA_ROPE // 2, 1) * s_up + pltpu.roll(x, MLA_ROPE // 2, 1) * s_dn


def _front_kernel(h_ref, hnext_ref, ing_ref, win_ref, convw_ref, convb_ref, clng_ref,
                  clnb_ref, pww_ref, pwb_ref, qng_ref, wuq_ref, kvng_ref, wk_ref, wvt_ref,
                  gq_ref, gk_ref, sglng_ref, sglnb_ref, sgw_ref, sgbm_ref,
                  rc_ref, rsu_ref, rsd_ref,
                  ycs_ref, q_ref, k_ref, vt_ref, zm_ref,
                  proj0_ref, ybuf_ref, zs_ref, *, steps_per_batch, normalise_input):
    f32, bf16 = jnp.float32, jnp.bfloat16
    step = pl.program_id(0)

    def project(h_tile):
        if normalise_input:
            h_tile = _rms(h_tile, ing_ref[...], D_MODEL).astype(bf16)
        return jnp.dot(h_tile, win_ref[...], preferred_element_type=f32)

    tile_rows = lambda t: h_ref[t * TM:(t + 1) * TM, :]

    def up_project(proj, t):
        cq = _rms(proj[:, OFF_CQ:OFF_CQ + Q_LORA], qng_ref[...], Q_LORA).astype(bf16)
        qf = jnp.dot(cq, wuq_ref[...], preferred_element_type=f32)
        ckv = _rms(proj[:, OFF_CKV:OFF_CKV + KV_LORA], kvng_ref[...], KV_LORA).astype(bf16)
        kf = jnp.dot(ckv, wk_ref[...], preferred_element_type=f32)
        vt = lax.dot_general(wvt_ref[...], ckv, (((1,), (1,)), ((), ())),
                             preferred_element_type=f32)
        vt_ref[t * TM // TK, :, (t * TM) % TK:(t * TM) % TK + TM] = vt.astype(bf16)
        return qf, kf

    def mix(proj, qf, kf, t):
        rows = slice(t * TM, (t + 1) * TM)
        col = lambda off, w: proj[:, off:off + w]

        v = _layer_norm(jax.nn.gelu(col(OFF_V, SG_WIDTH)), sglng_ref[...], sglnb_ref[...], SG_WIDTH)
        trow = lax.broadcasted_iota(jnp.int32, (SG_CHUNK, SG_CHUNK), 0)
        tcol = lax.broadcasted_iota(jnp.int32, (SG_CHUNK, SG_CHUNK), 1)
        wcat = jnp.concatenate(
            [jnp.where(tcol <= trow, sgw_ref[g], 0.0) for g in range(SG_HEADS)], axis=1).astype(bf16)
        lane = lax.broadcasted_iota(jnp.int32, (SG_CHUNK, SG_WIDTH), 1)
        mixed = []
        for c in range(TM // SG_CHUNK):
            vc = v[c * SG_CHUNK:(c + 1) * SG_CHUNK, :]
            vstack = jnp.concatenate(
                [jnp.where((lane >= g * SG_HEAD_DIM) & (lane < (g + 1) * SG_HEAD_DIM), vc, 0.0)
                 for g in range(SG_HEADS)], axis=0).astype(bf16)
            mixed.append(jnp.dot(wcat, vstack, preferred_element_type=f32) + sgbm_ref[...])
        ys = jax.nn.gelu(col(OFF_U, SG_WIDTH)) * jnp.concatenate(mixed, axis=0) * _silu(col(OFF_ZS, SG_WIDTH))
        ycs_ref[rows, CONV_WIDTH:CONV_WIDTH + SG_WIDTH] = ys.astype(bf16)

        ybuf_ref[CONV_HALO:CONV_HALO + TM, :] = col(OFF_A, CONV_WIDTH) * _sigmoid(col(OFF_GLU, CONV_WIDTH))
        first = CONV_HALO - (CONV_K - 1)
        for r in range(1, SUBLANES):
            zs_ref[r - 1, 0:TM + CONV_HALO - SUBLANES, :] = ybuf_ref[r:r + TM + CONV_HALO - SUBLANES, :]
        chunks = []
        for c0 in range(0, TM, CONV_ROWS):
            conv = jnp.zeros((CONV_ROWS, CONV_WIDTH), f32) + convb_ref[...]
            for tap in range(CONV_K):
                r, off = (tap + first) % SUBLANES, (tap + first) // SUBLANES * SUBLANES + c0
                src = ybuf_ref if r == 0 else zs_ref.at[r - 1]
                conv = conv + src[off:off + CONV_ROWS, :] * convw_ref[tap:tap + 1, :]
            chunks.append(_silu(_layer_norm(conv, clng_ref[...], clnb_ref[...], CONV_WIDTH)).astype(bf16))
        ybuf_ref[0:CONV_HALO, :] = ybuf_ref[TM:TM + CONV_HALO, :]
        yc = jnp.concatenate(chunks, axis=0)
        yc = jnp.dot(yc, pww_ref[...], preferred_element_type=f32) + pwb_ref[...]
        yc = yc * _silu(col(OFF_ZC, CONV_WIDTH))
        ycs_ref[rows, 0:CONV_WIDTH] = yc.astype(bf16)

        zm_ref[rows, :] = col(OFF_ZM, MLA_WIDTH).astype(bf16)
        rc, rsu, rsd = rc_ref[rows, :], rsu_ref[rows, :], rsd_ref[rows, :]
        qc = rc * (gq_ref[0:1, :] * LOG2E)
        qs = (rsu + rsd) * (gq_ref[1:2, :] * LOG2E)
        real = (lax.broadcasted_iota(jnp.int32, (1, HEAD_PAD), 1) < MLA_QK).astype(f32)
        for hd in range(MLA_HEADS):
            qh = qf[:, hd * HEAD_PAD:(hd + 1) * HEAD_PAD]
            ss = jnp.sum(qh * qh * real, axis=-1, keepdims=True)
            roped = qh * qc + pltpu.roll(qh, LANES - MLA_ROPE, 1) * qs
            q_ref[rows, hd * HEAD_PAD:(hd + 1) * HEAD_PAD] = (
                roped * lax.rsqrt(ss + MLA_QK * EPS)).astype(bf16)
        gk = gk_ref[...] * MLA_QK ** 0.5
        kr = col(OFF_KR, HEAD_PAD)
        ss_r = jnp.sum(kr * kr, axis=-1, keepdims=True)
        krr = _rope(kr * gk, rc, rsu, rsd)
        for hd in range(MLA_HEADS):
            kh = kf[:, hd * HEAD_PAD:(hd + 1) * HEAD_PAD]
            ss = jnp.sum(kh * kh, axis=-1, keepdims=True) + ss_r
            k_ref[rows, hd * HEAD_PAD:(hd + 1) * HEAD_PAD] = (
                (kh * gk + krr) * lax.rsqrt(ss + MLA_QK * EPS)).astype(bf16)

    @pl.when(step % steps_per_batch == 0)
    def _():
        ybuf_ref[0:CONV_HALO, :] = jnp.zeros((CONV_HALO, CONV_WIDTH), f32)

    @pl.when(step == 0)
    def _():
        proj0_ref[...] = project(tile_rows(0))

    projs = {0: proj0_ref, 1: project(tile_rows(1))}
    ups = {0: up_project(proj0_ref, 0)}
    for t in range(FRONT_TILES):
        if t + 1 < FRONT_TILES:
            ups[t + 1] = up_project(projs[t + 1], t + 1)
        if t + 2 < FRONT_TILES:
            projs[t + 2] = project(tile_rows(t + 2))
        mix(projs.pop(t), *ups.pop(t), t)
    proj0_ref[...] = project(hnext_ref[...])


def _col_max8(s):
    m8 = jnp.max(s.reshape(s.shape[0] // SUBLANES, SUBLANES, s.shape[1]), axis=0)
    for shift in (4, 2, 1):
        m8 = jnp.maximum(m8, pltpu.roll(m8, shift, 0))
    return m8


def _rows8(x, fn, stat8):
    x3 = x.reshape(x.shape[0] // SUBLANES, SUBLANES, x.shape[1])
    return fn(x3, stat8[None]).reshape(x.shape)


def _attn_kernel(q_ref, k_ref, vt_ref, o_ref, sa_ref, sb_ref, maxa_ref, maxb_ref, m_ref, acc_ref):
    f32, bf16 = jnp.float32, jnp.bfloat16
    qi = pl.program_id(1)
    n_full = qi // (TK // TQ)
    causal = (lax.broadcasted_iota(jnp.int32, (TQ, TQ), 0)
              <= lax.broadcasted_iota(jnp.int32, (TQ, TQ), 1))
    ones_rows = jnp.ones((ACC_ROWS - MLA_V, TK), bf16)
    heads = range(MLA_HEADS)
    hslab = lambda hh: slice(hh * HEAD_PAD, (hh + 1) * HEAD_PAD)
    vslab = lambda hh: slice(hh * MLA_V, (hh + 1) * MLA_V)

    def scores(j, s_ref, max_ref, hh):
        kb = k_ref[pl.ds(pl.multiple_of(j * TK, TK), TK), hslab(hh)]
        s = lax.dot_general(kb, q_ref[:, hslab(hh)], (((1,), (1,)), ((), ())),
                            preferred_element_type=f32)
        s_ref[hh] = s
        max_ref[hh] = _col_max8(s)

    def consume(j, s_ref, max_ref, hh, mode):
        if mode == "full":
            s, blk_max = s_ref[hh], max_ref[hh]
        elif mode == "diag_short":
            s = jnp.where(causal, s_ref[hh, 0:TQ, :], NEG_BIG)
            blk_max = _col_max8(s)
        else:
            s = jnp.concatenate([s_ref[hh, 0:TQ, :],
                                 jnp.where(causal, s_ref[hh, TQ:TK, :], NEG_BIG)], axis=0)
            blk_max = _col_max8(s)
        keys = s.shape[0]
        m = m_ref[hh]
        m_new = jnp.maximum(m, blk_max)
        alpha = jnp.exp2(m - m_new)
        p = _rows8(s, lambda x, st: jnp.exp2(x - st), m_new)
        m_ref[hh] = m_new
        lhs = jnp.concatenate([vt_ref[j, vslab(hh), 0:keys], ones_rows[:, 0:keys]], axis=0)
        pv = jnp.dot(lhs, p.astype(bf16), preferred_element_type=f32)
        acc_ref[hh] = _rows8(acc_ref[hh], lambda x, st: x * st, alpha) + pv

    def half_step(j, cur, nxt):
        for hh in heads:
            scores(j + 1, nxt[0], nxt[1], hh)
            consume(j, cur[0], cur[1], hh, "full")

    def diagonal(buf):
        @pl.when(qi % (TK // TQ) == 0)
        def _():
            for hh in heads:
                consume(n_full, buf[0], buf[1], hh, "diag_short")

        @pl.when(qi % (TK // TQ) == 1)
        def _():
            for hh in heads:
                consume(n_full, buf[0], buf[1], hh, "diag_long")

    buf_a, buf_b = (sa_ref, maxa_ref), (sb_ref, maxb_ref)
    m_ref[...] = jnp.full(m_ref.shape, NEG_BIG, f32)
    acc_ref[...] = jnp.zeros(acc_ref.shape, f32)
    for hh in heads:
        scores(0, sa_ref, maxa_ref, hh)

    def pair(j):
        half_step(j, buf_a, buf_b)
        half_step(j + 1, buf_b, buf_a)

    def two_pairs(jj, carry):
        pair(4 * jj)
        pair(4 * jj + 2)
        return carry

    lax.fori_loop(0, n_full // 4, two_pairs, 0)
    left = n_full % 4

    @pl.when(left >= 2)
    def _():
        pair(n_full - left)

    @pl.when(left % 2 == 1)
    def _():
        half_step(n_full - 1, buf_a, buf_b)
        diagonal(buf_b)

    @pl.when(left % 2 == 0)
    def _():
        diagonal(buf_a)

    out_t = jnp.concatenate(
        [_rows8(acc_ref[hh][:MLA_V], lambda x, st: x / st, acc_ref[hh][MLA_V:MLA_V + SUBLANES])
         for hh in heads], axis=0)
    o_ref[...] = out_t.T.astype(bf16)


def _back_kernel(x_ref, ycs_ref, ot_ref, zm_ref, bgc_ref, bgm_ref, bgs_ref, wout_ref, *rest):
    f32, bf16 = jnp.float32, jnp.bfloat16
    ym = ot_ref[...].astype(f32) * _silu(zm_ref[...].astype(f32))
    ycs = ycs_ref[...].astype(f32)
    y = jnp.concatenate(
        [_rms(ycs[:, 0:CONV_WIDTH], bgc_ref[...], CONV_WIDTH).astype(bf16),
         _rms(ym, bgm_ref[...], MLA_WIDTH).astype(bf16),
         _rms(ycs[:, CONV_WIDTH:], bgs_ref[...], SG_WIDTH).astype(bf16)], axis=1)
    x_new = x_ref[...] + jnp.dot(y, wout_ref[...], preferred_element_type=f32)
    if len(rest) == 1:
        rest[0][...] = x_new
    else:
        next_g_ref, out_ref, next_h_ref = rest
        out_ref[...] = x_new
        next_h_ref[...] = _rms(x_new, next_g_ref[...], D_MODEL).astype(bf16)


def _rope_lane_tables(seq):
    half = MLA_ROPE // 2
    inv_freq = ROPE_THETA ** (-jnp.arange(half, dtype=jnp.float32) / half)
    per_row = LANES // half
    pos = (jnp.arange(seq // per_row, dtype=jnp.int32)[:, None] * per_row
           + jnp.arange(LANES, dtype=jnp.int32)[None, :] // half).astype(jnp.float32)
    ang = pos * jnp.tile(inv_freq, per_row)[None, :]
    cos, sin = jnp.cos(ang).reshape(seq, half), jnp.sin(ang).reshape(seq, half)
    ones = jnp.ones((seq, MLA_NOPE), jnp.float32)
    zpad = jnp.zeros((seq, HEAD_PAD - MLA_QK), jnp.float32)
    zn = jnp.zeros((seq, MLA_NOPE), jnp.float32)
    zh = jnp.zeros((seq, half), jnp.float32)
    c = jnp.concatenate([ones, cos, cos, zpad], axis=1)
    s_up = jnp.concatenate([zn, -sin, zh, zpad], axis=1)
    s_dn = jnp.concatenate([zn, zh, sin, zpad], axis=1)
    return c, s_up, s_dn


def _prep_params(p):
    bf16 = jnp.bfloat16
    depth = p['w_in'].shape[0]
    half = MLA_ROPE // 2
    w_in = p['w_in']
    kr0 = 3 * CONV_WIDTH + Q_LORA + KV_LORA
    win = jnp.zeros((depth, D_MODEL, IN_COLS_PAD), bf16)
    win = win.at[:, :, :OFF_KR].set(w_in[:, :, :kr0].astype(bf16))
    win = win.at[:, :, OFF_KR + MLA_NOPE:OFF_KR + MLA_QK].set(w_in[:, :, kr0:kr0 + MLA_ROPE].astype(bf16))
    win = win.at[:, :, OFF_ZM:].set(w_in[:, :, kr0 + MLA_ROPE:].astype(bf16))
    wq = p['w_uq'].astype(bf16).reshape(depth, Q_LORA, MLA_HEADS, MLA_QK)
    wuq = jnp.concatenate([wq, wq[..., MLA_NOPE + half:], wq[..., MLA_NOPE:MLA_NOPE + half]],
                          axis=3).reshape(depth, Q_LORA, QK_PAD)
    wukv = p['w_ukv'].astype(bf16).reshape(depth, KV_LORA, MLA_HEADS, MLA_NOPE + MLA_V)
    wk = jnp.pad(wukv[..., :MLA_NOPE], ((0, 0), (0, 0), (0, 0), (0, HEAD_PAD - MLA_NOPE))
                 ).reshape(depth, KV_LORA, QK_PAD)
    wvt = jnp.swapaxes(wukv[..., MLA_NOPE:].reshape(depth, KV_LORA, MLA_WIDTH), 1, 2)
    pad_g = lambda g: jnp.pad(g, ((0, 0), (0, HEAD_PAD - g.shape[1])))[:, None, :]
    gq = p['qk_q_g']
    partner = jnp.concatenate([jnp.zeros((depth, MLA_NOPE), gq.dtype), gq[:, MLA_NOPE + half:],
                               gq[:, MLA_NOPE:MLA_NOPE + half]], axis=1)
    bng = p['branch_norm_g']
    vec = lambda a: a[:, None, :]
    return dict(
        win=win, wuq=wuq, wk=wk, wvt=wvt,
        gq=jnp.concatenate([pad_g(gq), pad_g(partner)], axis=1), gk=pad_g(p['qk_k_g']),
        sgbm=jnp.repeat(jnp.swapaxes(p['sg_b'], 1, 2), SG_HEAD_DIM, axis=2),
        pww=p['conv_pw_w'].astype(bf16), wout=p['w_out'].astype(bf16),
        conv_w=p['conv_w'], sg_w=p['sg_w'],
        conv_b=vec(p['conv_b']), conv_ln_g=vec(p['conv_ln_g']), conv_ln_b=vec(p['conv_ln_b']),
        conv_pw_b=vec(p['conv_pw_b']), q_norm_g=vec(p['q_norm_g']), kv_norm_g=vec(p['kv_norm_g']),
        sg_ln_g=vec(p['sg_ln_g']), sg_ln_b=vec(p['sg_ln_b']), norm_g=vec(p['norm_g']),
        bg_conv=vec(bng[:, :CONV_WIDTH]), bg_mla=vec(bng[:, CONV_WIDTH:CONV_WIDTH + MLA_WIDTH]),
        bg_sg=vec(bng[:, CONV_WIDTH + MLA_WIDTH:]))


def _layer(x, h, rope, w, layer, emit_next_h):
    B, S, _ = x.shape
    n_tok = B * S
    tstep = FRONT_TILES * TM
    n_steps = n_tok // tstep
    steps_per_batch = S // tstep
    bf16 = jnp.bfloat16

    def of_layer(name, which=layer):
        shape = w[name].shape[1:]
        return w[name], pl.BlockSpec((None,) + shape, lambda *_: (which,) + (0,) * len(shape),
                                     pipeline_mode=pl.Buffered(1))

    ropespec = pl.BlockSpec((tstep, HEAD_PAD), lambda s: (s % steps_per_batch, 0))
    front_in = [
        (h, pl.BlockSpec((tstep, D_MODEL), lambda s: (s, 0))),
        (h, pl.BlockSpec((TM, D_MODEL),
                         lambda s: (jnp.minimum(FRONT_TILES * (s + 1), n_tok // TM - 1), 0))),
        of_layer('norm_g'), of_layer('win'), of_layer('conv_w'), of_layer('conv_b'), of_layer('conv_ln_g'),
        of_layer('conv_ln_b'), of_layer('pww'), of_layer('conv_pw_b'), of_layer('q_norm_g'),
        of_layer('wuq'), of_layer('kv_norm_g'), of_layer('wk'), of_layer('wvt'),
        of_layer('gq'), of_layer('gk'), of_layer('sg_ln_g'), of_layer('sg_ln_b'),
        of_layer('sg_w'), of_layer('sgbm'),
        (rope[0], ropespec), (rope[1], ropespec), (rope[2], ropespec),
    ]
    otile = lambda width: pl.BlockSpec((tstep, width), lambda s: (s, 0))
    ycs, q, k, vt, zm = pl.pallas_call(
        functools.partial(_front_kernel, steps_per_batch=steps_per_batch,
                          normalise_input=h.dtype != bf16),
        grid=(n_steps,),
        in_specs=[s for _, s in front_in],
        out_specs=[otile(CONV_WIDTH + SG_WIDTH), otile(QK_PAD), otile(QK_PAD),
                   pl.BlockSpec((tstep // TK, MLA_WIDTH, TK), lambda s: (s, 0, 0)),
                   otile(MLA_WIDTH)],
        out_shape=[jax.ShapeDtypeStruct((n_tok, CONV_WIDTH + SG_WIDTH), bf16),
                   jax.ShapeDtypeStruct((n_tok, QK_PAD), bf16),
                   jax.ShapeDtypeStruct((n_tok, QK_PAD), bf16),
                   jax.ShapeDtypeStruct((n_tok // TK, MLA_WIDTH, TK), bf16),
                   jax.ShapeDtypeStruct((n_tok, MLA_WIDTH), bf16)],
        scratch_shapes=[pltpu.VMEM((TM, IN_COLS_PAD), jnp.float32),
                        pltpu.VMEM((CONV_HALO + TM, CONV_WIDTH), jnp.float32),
                        pltpu.VMEM((SUBLANES - 1, CONV_HALO + TM, CONV_WIDTH), jnp.float32)],
        compiler_params=pltpu.CompilerParams(
            dimension_semantics=("arbitrary",), vmem_limit_bytes=VMEM_LIMIT),
        name="front",
    )(*[a for a, _ in front_in])

    nq, nkv = S // TQ, S // TK
    ot = pl.pallas_call(
        _attn_kernel,
        grid=(B, nq),
        in_specs=[pl.BlockSpec((None, TQ, QK_PAD), lambda b, i: (b, i, 0)),
                  pl.BlockSpec((None, S, QK_PAD), lambda b, i: (b, 0, 0),
                               pipeline_mode=pl.Buffered(1)),
                  pl.BlockSpec((None, nkv, MLA_WIDTH, TK), lambda b, i: (b, 0, 0, 0),
                               pipeline_mode=pl.Buffered(1))],
        out_specs=pl.BlockSpec((None, TQ, MLA_WIDTH), lambda b, i: (b, i, 0)),
        out_shape=jax.ShapeDtypeStruct((B, S, MLA_WIDTH), bf16),
        scratch_shapes=[pltpu.VMEM((MLA_HEADS, TK, TQ), jnp.float32)] * 2
                       + [pltpu.VMEM((MLA_HEADS, SUBLANES, TQ), jnp.float32)] * 3
                       + [pltpu.VMEM((MLA_HEADS, ACC_ROWS, TQ), jnp.float32)],
        compiler_params=pltpu.CompilerParams(
            dimension_semantics=("arbitrary", "arbitrary"), vmem_limit_bytes=VMEM_LIMIT),
        name="attn",
    )(q.reshape(B, S, QK_PAD), k.reshape(B, S, QK_PAD), vt.reshape(B, nkv, MLA_WIDTH, TK))

    tile = lambda width: pl.BlockSpec((None, TB, width), lambda b, i: (b, i, 0))
    back_in = [(x, tile(D_MODEL)), (ycs.reshape(B, S, -1), tile(CONV_WIDTH + SG_WIDTH)),
               (ot, tile(MLA_WIDTH)), (zm.reshape(B, S, -1), tile(MLA_WIDTH)),
               of_layer('bg_conv'), of_layer('bg_mla'), of_layer('bg_sg'), of_layer('wout')]
    out_specs = [tile(D_MODEL)]
    out_shape = [jax.ShapeDtypeStruct((B, S, D_MODEL), jnp.float32)]
    if emit_next_h:
        back_in.append(of_layer('norm_g', layer + 1))
        out_specs.append(tile(D_MODEL))
        out_shape.append(jax.ShapeDtypeStruct((B, S, D_MODEL), bf16))
    outs = pl.pallas_call(
        _back_kernel,
        grid=(B, S // TB),
        in_specs=[s for _, s in back_in],
        out_specs=out_specs,
        out_shape=out_shape,
        compiler_params=pltpu.CompilerParams(
            dimension_semantics=("arbitrary", "arbitrary"), vmem_limit_bytes=VMEM_LIMIT),
        name="back",
    )(*[a for a, _ in back_in])
    return outs[0], (outs[1].reshape(n_tok, D_MODEL) if emit_next_h else None)


def kernel(x, norm_g, w_in, conv_w, conv_b, conv_ln_g, conv_ln_b, conv_pw_w, conv_pw_b,
           q_norm_g, w_uq, kv_norm_g, w_ukv, qk_q_g, qk_k_g, sg_ln_g, sg_ln_b, sg_w, sg_b,
           branch_norm_g, w_out):
    params = dict(norm_g=norm_g, w_in=w_in, conv_w=conv_w, conv_b=conv_b, conv_ln_g=conv_ln_g,
                  conv_ln_b=conv_ln_b, conv_pw_w=conv_pw_w, conv_pw_b=conv_pw_b,
                  q_norm_g=q_norm_g, w_uq=w_uq, kv_norm_g=kv_norm_g, w_ukv=w_ukv,
                  qk_q_g=qk_q_g, qk_k_g=qk_k_g, sg_ln_g=sg_ln_g, sg_ln_b=sg_ln_b, sg_w=sg_w,
                  sg_b=sg_b, branch_norm_g=branch_norm_g, w_out=w_out)
    rope = _rope_lane_tables(x.shape[1])
    depth = norm_g.shape[0]
    w = _prep_params(params)
    h = x.reshape(-1, D_MODEL)
    for layer in range(depth):
        x, h = _layer(x, h, rope, w, layer, layer + 1 < depth)
    return x
```

```python
import functools

import jax
import jax.numpy as jnp
from jax import lax
from jax.experimental import pallas as pl
from jax.experimental.pallas import tpu as pltpu

D_MODEL = 1024
CONV_WIDTH = 256
CONV_K = 31
MLA_HEADS = 8
MLA_NOPE = 64
MLA_ROPE = 32
MLA_QK = MLA_NOPE + MLA_ROPE
MLA_V = 64
MLA_WIDTH = MLA_HEADS * MLA_V
Q_LORA = 768
KV_LORA = 256
ROPE_THETA = 10000.0
SG_WIDTH = 256
SG_HEADS = 4
SG_HEAD_DIM = SG_WIDTH // SG_HEADS
SG_CHUNK = 128
EPS = 1e-6

SUBLANES = 8
LANES = 128
HEAD_PAD = LANES
QK_PAD = MLA_HEADS * HEAD_PAD

OFF_A = 0
OFF_GLU = OFF_A + CONV_WIDTH
OFF_ZC = OFF_GLU + CONV_WIDTH
OFF_CQ = OFF_ZC + CONV_WIDTH
OFF_CKV = OFF_CQ + Q_LORA
OFF_KR = OFF_CKV + KV_LORA
OFF_ZM = OFF_KR + HEAD_PAD
OFF_U = OFF_ZM + MLA_WIDTH
OFF_V = OFF_U + SG_WIDTH
OFF_ZS = OFF_V + SG_WIDTH
IN_COLS_PAD = OFF_ZS + SG_WIDTH

CONV_HALO = 32
CONV_ROWS = 64
TM = 256
TB = 512
FRONT_TILES = 4
TQ = 256
TK = 512
BF16_ROWS = 2 * SUBLANES
ACC_ROWS = MLA_V + BF16_ROWS
NEG_BIG = -1e30
LOG2E = 1.4426950408889634
VMEM_LIMIT = 48 * 1024 * 1024

assert FRONT_TILES >= 2 and (FRONT_TILES * TM) % TK == 0 and TK % TM == 0 and TM % SG_CHUNK == 0 and TK == 2 * TQ


def _rms(x, g, width):
    ss = jnp.sum(x * x, axis=-1, keepdims=True)
    return x * lax.rsqrt(ss + width * EPS) * (g * width ** 0.5)


def _layer_norm(x, g, b, width):
    mu = jnp.sum(x, axis=-1, keepdims=True) * (1.0 / width)
    xc = x - mu
    ss = jnp.sum(xc * xc, axis=-1, keepdims=True)
    return xc * lax.rsqrt(ss + width * EPS) * (g * width ** 0.5) + b


def _sigmoid(x):
    return 0.5 * jnp.tanh(0.5 * x) + 0.5


def _silu(x):
    return x * _sigmoid(x)


def _rope(x, c, s_up, s_dn):
    return x * c + pltpu.roll(x, LANES - MLA_ROPE // 2, 1) * s_up + pltpu.roll(x, MLA_ROPE // 2, 1) * s_dn


def _front_kernel(h_ref, hnext_ref, ing_ref, win_ref, convw_ref, convb_ref, clng_ref,
                  clnb_ref, pww_ref, pwb_ref, qng_ref, wuq_ref, kvng_ref, wk_ref, wvt_ref,
                  gq_ref, gk_ref, sglng_ref, sglnb_ref, sgw_ref, sgbm_ref,
                  rc_ref, rsu_ref, rsd_ref,
                  ycs_ref, q_ref, k_ref, vt_ref, zm_ref,
                  proj0_ref, ybuf_ref, zs_ref, *, steps_per_batch, normalise_input):
    f32, bf16 = jnp.float32, jnp.bfloat16
    step = pl.program_id(0)

    def project(h_tile):
        if normalise_input:
            h_tile = _rms(h_tile, ing_ref[...], D_MODEL).astype(bf16)
        return jnp.dot(h_tile, win_ref[...], preferred_element_type=f32)

    tile_rows = lambda t: h_ref[t * TM:(t + 1) * TM, :]

    def up_project(proj, t):
        cq = _rms(proj[:, OFF_CQ:OFF_CQ + Q_LORA], qng_ref[...], Q_LORA).astype(bf16)
        qf = jnp.dot(cq, wuq_ref[...], preferred_element_type=f32)
        ckv = _rms(proj[:, OFF_CKV:OFF_CKV + KV_LORA], kvng_ref[...], KV_LORA).astype(bf16)
        kf = jnp.dot(ckv, wk_ref[...], preferred_element_type=f32)
        vt = lax.dot_general(wvt_ref[...], ckv, (((1,), (1,)), ((), ())),
                             preferred_element_type=f32)
        vt_ref[t * TM // TK, :, (t * TM) % TK:(t * TM) % TK + TM] = vt.astype(bf16)
        return qf, kf

    def mix(proj, qf, kf, t):
        rows = slice(t * TM, (t + 1) * TM)
        col = lambda off, w: proj[:, off:off + w]

        v = _layer_norm(jax.nn.gelu(col(OFF_V, SG_WIDTH)), sglng_ref[...], sglnb_ref[...], SG_WIDTH)
        mixed = []
        for c in range(TM // SG_CHUNK):
            vc = v[c * SG_CHUNK:(c + 1) * SG_CHUNK, :]
            vstack = jnp.concatenate(
                [jnp.where((lane >= g * SG_HEAD_DIM) & (lane < (g + 1) * SG_HEAD_DIM), vc, 0.0)
                 for g in range(SG_HEADS)], axis=0).astype(bf16)
            mixed.append(jnp.dot(wcat, vstack, preferred_element_type=f32) + sgbm_ref[...])
        ys = jax.nn.gelu(col(OFF_U, SG_WIDTH)) * jnp.concatenate(mixed, axis=0) * _silu(col(OFF_ZS, SG_WIDTH))
        ycs_ref[rows, CONV_WIDTH:CONV_WIDTH + SG_WIDTH] = ys.astype(bf16)

        ybuf_ref[CONV_HALO:CONV_HALO + TM, :] = col(OFF_A, CONV_WIDTH) * _sigmoid(col(OFF_GLU, CONV_WIDTH))
        first = CONV_HALO - (CONV_K - 1)
        for r in range(1, SUBLANES):
            zs_ref[r - 1, 0:TM + CONV_HALO - SUBLANES, :] = ybuf_ref[r:r + TM + CONV_HALO - SUBLANES, :]
        chunks = []
        for c0 in range(0, TM, CONV_ROWS):
            conv = jnp.zeros((CONV_ROWS, CONV_WIDTH), f32) + convb_ref[...]
            for tap in range(CONV_K):
                r, off = (tap + first) % SUBLANES, (tap + first) // SUBLANES * SUBLANES + c0
                src = ybuf_ref if r == 0 else zs_ref.at[r - 1]
                conv = conv + src[off:off + CONV_ROWS, :] * convw_ref[tap:tap + 1, :]
            chunks.append(_silu(_layer_norm(conv, clng_ref[...], clnb_ref[...], CONV_WIDTH)).astype(bf16))
        ybuf_ref[0:CONV_HALO, :] = ybuf_ref[TM:TM + CONV_HALO, :]
        yc = jnp.concatenate(chunks, axis=0)
        yc = jnp.dot(yc, pww_ref[...], preferred_element_type=f32) + pwb_ref[...]
        yc = yc * _silu(col(OFF_ZC, CONV_WIDTH))
        ycs_ref[rows, 0:CONV_WIDTH] = yc.astype(bf16)

        zm_ref[rows, :] = col(OFF_ZM, MLA_WIDTH).astype(bf16)
        rc, rsu, rsd = rc_ref[rows, :], rsu_ref[rows, :], rsd_ref[rows, :]
        qc = rc * (gq_ref[0:1, :] * LOG2E)
        qs = (rsu + rsd) * (gq_ref[1:2, :] * LOG2E)
        real = (lax.broadcasted_iota(jnp.int32, (1, HEAD_PAD), 1) < MLA_QK).astype(f32)
        for hd in range(MLA_HEADS):
            qh = qf[:, hd * HEAD_PAD:(hd + 1) * HEAD_PAD]
            ss = jnp.sum(qh * qh * real, axis=-1, keepdims=True)
            roped = qh * qc + pltpu.roll(qh, LANES - MLA_ROPE, 1) * qs
            q_ref[rows, hd * HEAD_PAD:(hd + 1) * HEAD_PAD] = (
                roped * lax.rsqrt(ss + MLA_QK * EPS)).astype(bf16)
        gk = gk_ref[...] * MLA_QK ** 0.5
        kr = col(OFF_KR, HEAD_PAD)
        ss_r = jnp.sum(kr * kr, axis=-1, keepdims=True)
        krr = _rope(kr * gk, rc, rsu, rsd)
        for hd in range(MLA_HEADS):
            kh = kf[:, hd * HEAD_PAD:(hd + 1) * HEAD_PAD]
            ss = jnp.sum(kh * kh, axis=-1, keepdims=True) + ss_r
            k_ref[rows, hd * HEAD_PAD:(hd + 1) * HEAD_PAD] = (
                (kh * gk + krr) * lax.rsqrt(ss + MLA_QK * EPS)).astype(bf16)

    trow = lax.broadcasted_iota(jnp.int32, (SG_CHUNK, SG_CHUNK), 0)
    tcol = lax.broadcasted_iota(jnp.int32, (SG_CHUNK, SG_CHUNK), 1)
    wcat = jnp.concatenate(
        [jnp.where(tcol <= trow, sgw_ref[g], 0.0) for g in range(SG_HEADS)], axis=1).astype(bf16)
    lane = lax.broadcasted_iota(jnp.int32, (SG_CHUNK, SG_WIDTH), 1)

    @pl.when(step % steps_per_batch == 0)
    def _():
        ybuf_ref[0:CONV_HALO, :] = jnp.zeros((CONV_HALO, CONV_WIDTH), f32)

    @pl.when(step == 0)
    def _():
        proj0_ref[...] = project(tile_rows(0))

    projs = {0: proj0_ref, 1: project(tile_rows(1))}
    ups = {0: up_project(proj0_ref, 0)}
    for t in range(FRONT_TILES):
        if t + 1 < FRONT_TILES:
            ups[t + 1] = up_project(projs[t + 1], t + 1)
        if t + 2 < FRONT_TILES:
            projs[t + 2] = project(tile_rows(t + 2))
        mix(projs.pop(t), *ups.pop(t), t)
    proj0_ref[...] = project(hnext_ref[...])


def _col_max8(s):
    m8 = jnp.max(s.reshape(s.shape[0] // SUBLANES, SUBLANES, s.shape[1]), axis=0)
    for shift in (4, 2, 1):
        m8 = jnp.maximum(m8, pltpu.roll(m8, shift, 0))
    return m8


def _rows8(x, fn, stat8):
    x3 = x.reshape(x.shape[0] // SUBLANES, SUBLANES, x.shape[1])
    return fn(x3, stat8[None]).reshape(x.shape)


def _attn_kernel(q_ref, k_ref, vt_ref, o_ref, sa_ref, sb_ref, maxa_ref, maxb_ref, m_ref, acc_ref):
    f32, bf16 = jnp.float32, jnp.bfloat16
    qi = pl.program_id(1)
    n_full = qi // (TK // TQ)
    causal = (lax.broadcasted_iota(jnp.int32, (TQ, TQ), 0)
              <= lax.broadcasted_iota(jnp.int32, (TQ, TQ), 1))
    ones_rows = jnp.ones((ACC_ROWS - MLA_V, TK), bf16)
    heads = range(MLA_HEADS)
    hslab = lambda hh: slice(hh * HEAD_PAD, (hh + 1) * HEAD_PAD)
    vslab = lambda hh: slice(hh * MLA_V, (hh + 1) * MLA_V)

    def scores(j, s_ref, max_ref, hh):
        kb = k_ref[pl.ds(pl.multiple_of(j * TK, TK), TK), hslab(hh)]
        s = lax.dot_general(kb, q_ref[:, hslab(hh)], (((1,), (1,)), ((), ())),
                            preferred_element_type=f32)
        s_ref[hh] = s
        max_ref[hh] = _col_max8(s)

    def consume(j, s_ref, max_ref, hh, mode):
        if mode == "full":
            s, blk_max = s_ref[hh], max_ref[hh]
        elif mode == "diag_short":
            s = jnp.where(causal, s_ref[hh, 0:TQ, :], NEG_BIG)
            blk_max = _col_max8(s)
        else:
            s = jnp.concatenate([s_ref[hh, 0:TQ, :],
                                 jnp.where(causal, s_ref[hh, TQ:TK, :], NEG_BIG)], axis=0)
            blk_max = _col_max8(s)
        keys = s.shape[0]
        m = m_ref[hh]
        m_new = jnp.maximum(m, blk_max)
        alpha = jnp.exp2(m - m_new)
        p = _rows8(s, lambda x, st: jnp.exp2(x - st), m_new)
        m_ref[hh] = m_new
        lhs = jnp.concatenate([vt_ref[j, vslab(hh), 0:keys], ones_rows[:, 0:keys]], axis=0)
        pv = jnp.dot(lhs, p.astype(bf16), preferred_element_type=f32)
        acc_ref[hh] = _rows8(acc_ref[hh], lambda x, st: x * st, alpha) + pv

    def half_step(j, cur, nxt):
        for hh in heads:
            scores(j + 1, nxt[0], nxt[1], hh)
            consume(j, cur[0], cur[1], hh, "full")

    def diagonal(buf):
        @pl.when(qi % (TK // TQ) == 0)
        def _():
            for hh in heads:
                consume(n_full, buf[0], buf[1], hh, "diag_short")

        @pl.when(qi % (TK // TQ) == 1)
        def _():
            for hh in heads:
                consume(n_full, buf[0], buf[1], hh, "diag_long")

    buf_a, buf_b = (sa_ref, maxa_ref), (sb_ref, maxb_ref)
    m_ref[...] = jnp.full(m_ref.shape, NEG_BIG, f32)
    acc_ref[...] = jnp.zeros(acc_ref.shape, f32)
    for hh in heads:
        scores(0, sa_ref, maxa_ref, hh)

    def pair(j):
        half_step(j, buf_a, buf_b)
        half_step(j + 1, buf_b, buf_a)

    def two_pairs(jj, carry):
        pair(4 * jj)
        pair(4 * jj + 2)
        return carry

    lax.fori_loop(0, n_full // 4, two_pairs, 0)
    left = n_full % 4

    @pl.when(left >= 2)
    def _():
        pair(n_full - left)

    @pl.when(left % 2 == 1)
    def _():
        half_step(n_full - 1, buf_a, buf_b)
        diagonal(buf_b)

    @pl.when(left % 2 == 0)
    def _():
        diagonal(buf_a)

    out_t = jnp.concatenate(
        [_rows8(acc_ref[hh][:MLA_V], lambda x, st: x / st, acc_ref[hh][MLA_V:MLA_V + SUBLANES])
         for hh in heads], axis=0)
    o_ref[...] = out_t.T.astype(bf16)


def _back_kernel(x_ref, ycs_ref, ot_ref, zm_ref, bgc_ref, bgm_ref, bgs_ref, wout_ref, *rest):
    f32, bf16 = jnp.float32, jnp.bfloat16
    ym = ot_ref[...].astype(f32) * _silu(zm_ref[...].astype(f32))
    ycs = ycs_ref[...].astype(f32)
    y = jnp.concatenate(
        [_rms(ycs[:, 0:CONV_WIDTH], bgc_ref[...], CONV_WIDTH).astype(bf16),
         _rms(ym, bgm_ref[...], MLA_WIDTH).astype(bf16),
         _rms(ycs[:, CONV_WIDTH:], bgs_ref[...], SG_WIDTH).astype(bf16)], axis=1)
    x_new = x_ref[...] + jnp.dot(y, wout_ref[...], preferred_element_type=f32)
    if len(rest) == 1:
        rest[0][...] = x_new
    else:
        next_g_ref, out_ref, next_h_ref = rest
        out_ref[...] = x_new
        next_h_ref[...] = _rms(x_new, next_g_ref[...], D_MODEL).astype(bf16)


def _rope_lane_tables(seq):
    half = MLA_ROPE // 2
    inv_freq = ROPE_THETA ** (-jnp.arange(half, dtype=jnp.float32) / half)
    per_row = LANES // half
    pos = (jnp.arange(seq // per_row, dtype=jnp.int32)[:, None] * per_row
           + jnp.arange(LANES, dtype=jnp.int32)[None, :] // half).astype(jnp.float32)
    ang = pos * jnp.tile(inv_freq, per_row)[None, :]
    cos, sin = lax.optimization_barrier((jnp.cos(ang), jnp.sin(ang)))
    cos, sin = cos.reshape(seq, half), sin.reshape(seq, half)
    ones = jnp.ones((seq, MLA_NOPE), jnp.float32)
    zpad = jnp.zeros((seq, HEAD_PAD - MLA_QK), jnp.float32)
    zn = jnp.zeros((seq, MLA_NOPE), jnp.float32)
    zh = jnp.zeros((seq, half), jnp.float32)
    c = jnp.concatenate([ones, cos, cos, zpad], axis=1)
    s_up = jnp.concatenate([zn, -sin, zh, zpad], axis=1)
    s_dn = jnp.concatenate([zn, zh, sin, zpad], axis=1)
    return c, s_up, s_dn


def _prep_params(p):
    bf16 = jnp.bfloat16
    depth = p['w_in'].shape[0]
    half = MLA_ROPE // 2
    w_in = p['w_in']
    kr0 = 3 * CONV_WIDTH + Q_LORA + KV_LORA
    win = jnp.zeros((depth, D_MODEL, IN_COLS_PAD), bf16)
    win = win.at[:, :, :OFF_KR].set(w_in[:, :, :kr0].astype(bf16))
    win = win.at[:, :, OFF_KR + MLA_NOPE:OFF_KR + MLA_QK].set(w_in[:, :, kr0:kr0 + MLA_ROPE].astype(bf16))
    win = win.at[:, :, OFF_ZM:].set(w_in[:, :, kr0 + MLA_ROPE:].astype(bf16))
    wq = p['w_uq'].astype(bf16).reshape(depth, Q_LORA, MLA_HEADS, MLA_QK)
    wuq = jnp.concatenate([wq, wq[..., MLA_NOPE + half:], wq[..., MLA_NOPE:MLA_NOPE + half]],
                          axis=3).reshape(depth, Q_LORA, QK_PAD)
    wukv = p['w_ukv'].astype(bf16).reshape(depth, KV_LORA, MLA_HEADS, MLA_NOPE + MLA_V)
    wk = jnp.pad(wukv[..., :MLA_NOPE], ((0, 0), (0, 0), (0, 0), (0, HEAD_PAD - MLA_NOPE))
                 ).reshape(depth, KV_LORA, QK_PAD)
    wvt = jnp.swapaxes(wukv[..., MLA_NOPE:].reshape(depth, KV_LORA, MLA_WIDTH), 1, 2)
    pad_g = lambda g: jnp.pad(g, ((0, 0), (0, HEAD_PAD - g.shape[1])))[:, None, :]
    gq = p['qk_q_g']
    partner = jnp.concatenate([jnp.zeros((depth, MLA_NOPE), gq.dtype), gq[:, MLA_NOPE + half:],
                               gq[:, MLA_NOPE:MLA_NOPE + half]], axis=1)
    bng = p['branch_norm_g']
    vec = lambda a: a[:, None, :]
    return dict(
        win=win, wuq=wuq, wk=wk, wvt=wvt,
        gq=jnp.concatenate([pad_g(gq), pad_g(partner)], axis=1), gk=pad_g(p['qk_k_g']),
        sgbm=jnp.repeat(jnp.swapaxes(p['sg_b'], 1, 2), SG_HEAD_DIM, axis=2),
        pww=p['conv_pw_w'].astype(bf16), wout=p['w_out'].astype(bf16),
        conv_w=p['conv_w'], sg_w=p['sg_w'],
        conv_b=vec(p['conv_b']), conv_ln_g=vec(p['conv_ln_g']), conv_ln_b=vec(p['conv_ln_b']),
        conv_pw_b=vec(p['conv_pw_b']), q_norm_g=vec(p['q_norm_g']), kv_norm_g=vec(p['kv_norm_g']),
        sg_ln_g=vec(p['sg_ln_g']), sg_ln_b=vec(p['sg_ln_b']), norm_g=vec(p['norm_g']),
        bg_conv=vec(bng[:, :CONV_WIDTH]), bg_mla=vec(bng[:, CONV_WIDTH:CONV_WIDTH + MLA_WIDTH]),
        bg_sg=vec(bng[:, CONV_WIDTH + MLA_WIDTH:]))


def _layer(x, h, rope, w, layer, emit_next_h):
    B, S, _ = x.shape
    n_tok = B * S
    tstep = FRONT_TILES * TM
    n_steps = n_tok // tstep
    steps_per_batch = S // tstep
    bf16 = jnp.bfloat16

    def of_layer(name, which=layer):
        shape = w[name].shape[1:]
        return w[name], pl.BlockSpec((None,) + shape, lambda *_: (which,) + (0,) * len(shape),
                                     pipeline_mode=pl.Buffered(1))

    ropespec = pl.BlockSpec((tstep, HEAD_PAD), lambda s: (s % steps_per_batch, 0))
    front_in = [
        (h, pl.BlockSpec((tstep, D_MODEL), lambda s: (s, 0))),
        (h, pl.BlockSpec((TM, D_MODEL),
                         lambda s: (jnp.minimum(FRONT_TILES * (s + 1), n_tok // TM - 1), 0))),
        of_layer('norm_g'), of_layer('win'), of_layer('conv_w'), of_layer('conv_b'), of_layer('conv_ln_g'),
        of_layer('conv_ln_b'), of_layer('pww'), of_layer('conv_pw_b'), of_layer('q_norm_g'),
        of_layer('wuq'), of_layer('kv_norm_g'), of_layer('wk'), of_layer('wvt'),
        of_layer('gq'), of_layer('gk'), of_layer('sg_ln_g'), of_layer('sg_ln_b'),
        of_layer('sg_w'), of_layer('sgbm'),
        (rope[0], ropespec), (rope[1], ropespec), (rope[2], ropespec),
    ]
    otile = lambda width: pl.BlockSpec((tstep, width), lambda s: (s, 0))
    ycs, q, k, vt, zm = pl.pallas_call(
        functools.partial(_front_kernel, steps_per_batch=steps_per_batch,
                          normalise_input=h.dtype != bf16),
        grid=(n_steps,),
        in_specs=[s for _, s in front_in],
        out_specs=[otile(CONV_WIDTH + SG_WIDTH), otile(QK_PAD), otile(QK_PAD),
                   pl.BlockSpec((tstep // TK, MLA_WIDTH, TK), lambda s: (s, 0, 0)),
                   otile(MLA_WIDTH)],
        out_shape=[jax.ShapeDtypeStruct((n_tok, CONV_WIDTH + SG_WIDTH), bf16),
                   jax.ShapeDtypeStruct((n_tok, QK_PAD), bf16),
                   jax.ShapeDtypeStruct((n_tok, QK_PAD), bf16),
                   jax.ShapeDtypeStruct((n_tok // TK, MLA_WIDTH, TK), bf16),
                   jax.ShapeDtypeStruct((n_tok, MLA_WIDTH), bf16)],
        scratch_shapes=[pltpu.VMEM((TM, IN_COLS_PAD), jnp.float32),
                        pltpu.VMEM((CONV_HALO + TM, CONV_WIDTH), jnp.float32),
                        pltpu.VMEM((SUBLANES - 1, CONV_HALO + TM, CONV_WIDTH), jnp.float32)],
        compiler_params=pltpu.CompilerParams(
            dimension_semantics=("arbitrary",), vmem_limit_bytes=VMEM_LIMIT),
        name="front",
    )(*[a for a, _ in front_in])

    nq, nkv = S // TQ, S // TK
    ot = pl.pallas_call(
        _attn_kernel,
        grid=(B, nq),
        in_specs=[pl.BlockSpec((None, TQ, QK_PAD), lambda b, i: (b, i, 0)),
                  pl.BlockSpec((None, S, QK_PAD), lambda b, i: (b, 0, 0),
                               pipeline_mode=pl.Buffered(1)),
                  pl.BlockSpec((None, nkv, MLA_WIDTH, TK), lambda b, i: (b, 0, 0, 0),
                               pipeline_mode=pl.Buffered(1))],
        out_specs=pl.BlockSpec((None, TQ, MLA_WIDTH), lambda b, i: (b, i, 0)),
        out_shape=jax.ShapeDtypeStruct((B, S, MLA_WIDTH), bf16),
        scratch_shapes=[pltpu.VMEM((MLA_HEADS, TK, TQ), jnp.float32)] * 2
                       + [pltpu.VMEM((MLA_HEADS, SUBLANES, TQ), jnp.float32)] * 3
                       + [pltpu.VMEM((MLA_HEADS, ACC_ROWS, TQ), jnp.float32)],
        compiler_params=pltpu.CompilerParams(
            dimension_semantics=("arbitrary", "arbitrary"), vmem_limit_bytes=VMEM_LIMIT),
        name="attn",
    )(q.reshape(B, S, QK_PAD), k.reshape(B, S, QK_PAD), vt.reshape(B, nkv, MLA_WIDTH, TK))

    tile = lambda width: pl.BlockSpec((None, TB, width), lambda b, i: (b, i, 0))
    back_in = [(x, tile(D_MODEL)), (ycs.reshape(B, S, -1), tile(CONV_WIDTH + SG_WIDTH)),
               (ot, tile(MLA_WIDTH)), (zm.reshape(B, S, -1), tile(MLA_WIDTH)),
               of_layer('bg_conv'), of_layer('bg_mla'), of_layer('bg_sg'), of_layer('wout')]
    out_specs = [tile(D_MODEL)]
    out_shape = [jax.ShapeDtypeStruct((B, S, D_MODEL), jnp.float32)]
    if emit_next_h:
        back_in.append(of_layer('norm_g', layer + 1))
        out_specs.append(tile(D_MODEL))
        out_shape.append(jax.ShapeDtypeStruct((B, S, D_MODEL), bf16))
    outs = pl.pallas_call(
        _back_kernel,
        grid=(B, S // TB),
        in_specs=[s for _, s in back_in],
        out_specs=out_specs,
        out_shape=out_shape,
        compiler_params=pltpu.CompilerParams(
            dimension_semantics=("arbitrary", "arbitrary"), vmem_limit_bytes=VMEM_LIMIT),
        name="back",
    )(*[a for a, _ in back_in])
    return outs[0], (outs[1].reshape(n_tok, D_MODEL) if emit_next_h else None)


def kernel(x, norm_g, w_in, conv_w, conv_b, conv_ln_g, conv_ln_b, conv_pw_w, conv_pw_b,
           q_norm_g, w_uq, kv_norm_g, w_ukv, qk_q_g, qk_k_g, sg_ln_g, sg_ln_b, sg_w, sg_b,
           branch_norm_g, w_out):
    params = dict(norm_g=norm_g, w_in=w_in, conv_w=conv_w, conv_b=conv_b, conv_ln_g=conv_ln_g,
                  conv_ln_b=conv_ln_b, conv_pw_w=conv_pw_w, conv_pw_b=conv_pw_b,
                  q_norm_g=q_norm_g, w_uq=w_uq, kv_norm_g=kv_norm_g, w_ukv=w_ukv,
                  qk_q_g=qk_q_g, qk_k_g=qk_k_g, sg_ln_g=sg_ln_g, sg_ln_b=sg_ln_b, sg_w=sg_w,
                  sg_b=sg_b, branch_norm_g=branch_norm_g, w_out=w_out)
    rope = _rope_lane_tables(x.shape[1])
    depth = norm_g.shape[0]
    w = _prep_params(params)
    h = x.reshape(-1, D_MODEL)
    for layer in range(depth):
        x, h = _layer(x, h, rope, w, layer, layer + 1 < depth)
    return x
```

```python
import functools

import jax
import jax.numpy as jnp
from jax import lax
from jax.experimental import pallas as pl
from jax.experimental.pallas import tpu as pltpu

D_MODEL = 1024
CONV_WIDTH = 256
CONV_K = 31
MLA_HEADS = 8
MLA_NOPE = 64
MLA_ROPE = 32
MLA_QK = MLA_NOPE + MLA_ROPE
MLA_V = 64
MLA_WIDTH = MLA_HEADS * MLA_V
Q_LORA = 768
KV_LORA = 256
ROPE_THETA = 10000.0
SG_WIDTH = 256
SG_HEADS = 4
SG_HEAD_DIM = SG_WIDTH // SG_HEADS
SG_CHUNK = 128
EPS = 1e-6

SUBLANES = 8
LANES = 128
HEAD_PAD = LANES
QK_PAD = MLA_HEADS * HEAD_PAD

OFF_A = 0
OFF_GLU = OFF_A + CONV_WIDTH
OFF_ZC = OFF_GLU + CONV_WIDTH
OFF_CQ = OFF_ZC + CONV_WIDTH
OFF_CKV = OFF_CQ + Q_LORA
OFF_KR = OFF_CKV + KV_LORA
OFF_ZM = OFF_KR + HEAD_PAD
OFF_U = OFF_ZM + MLA_WIDTH
OFF_V = OFF_U + SG_WIDTH
OFF_ZS = OFF_V + SG_WIDTH
IN_COLS_PAD = OFF_ZS + SG_WIDTH

CONV_HALO = 32
CONV_ROWS = 64
TM = 256
TB = 512
FRONT_TILES = 4
TQ = 256
TK = 512
BF16_ROWS = 2 * SUBLANES
ACC_ROWS = MLA_V + BF16_ROWS
NEG_BIG = -1e30
LOG2E = 1.4426950408889634
VMEM_LIMIT = 48 * 1024 * 1024

assert FRONT_TILES >= 2 and (FRONT_TILES * TM) % TK == 0 and TK % TM == 0 and TM % SG_CHUNK == 0 and TK == 2 * TQ


def _rms(x, g, width):
    ss = jnp.sum(x * x, axis=-1, keepdims=True)
    return x * lax.rsqrt(ss + width * EPS) * (g * width ** 0.5)


def _layer_norm(x, g, b, width):
    mu = jnp.sum(x, axis=-1, keepdims=True) * (1.0 / width)
    xc = x - mu
    ss = jnp.sum(xc * xc, axis=-1, keepdims=True)
    return xc * lax.rsqrt(ss + width * EPS) * (g * width ** 0.5) + b


def _sigmoid(x):
    return 0.5 * jnp.tanh(0.5 * x) + 0.5


def _silu(x):
    return x * _sigmoid(x)


def _rope(x, c, s_up, s_dn):
    return x * c + pltpu.roll(x, LANES - MLA_ROPE // 2, 1) * s_up + pltpu.roll(x, MLA_ROPE // 2, 1) * s_dn


def _front_kernel(h_ref, hnext_ref, ing_ref, win_ref, convw_ref, convb_ref, clng_ref,
                  clnb_ref, pww_ref, pwb_ref, qng_ref, wuq_ref, kvng_ref, wk_ref, wvt_ref,
                  gq_ref, gk_ref, sglng_ref, sglnb_ref, sgw_ref, sgbm_ref,
                  rc_ref, rsu_ref, rsd_ref,
                  ycs_ref, q_ref, k_ref, vt_ref, zm_ref,
                  proj0_ref, ybuf_ref, zs_ref, *, steps_per_batch, normalise_input):
    f32, bf16 = jnp.float32, jnp.bfloat16
    step = pl.program_id(0)

    def project(h_tile):
        if normalise_input:
            h_tile = _rms(h_tile, ing_ref[...], D_MODEL).astype(bf16)
        return jnp.dot(h_tile, win_ref[...], preferred_element_type=f32)

    tile_rows = lambda t: h_ref[t * TM:(t + 1) * TM, :]

    def up_project(proj, t):
        cq = _rms(proj[:, OFF_CQ:OFF_CQ + Q_LORA], qng_ref[...], Q_LORA).astype(bf16)
        qf = jnp.dot(cq, wuq_ref[...], preferred_element_type=f32)
        ckv = _rms(proj[:, OFF_CKV:OFF_CKV + KV_LORA], kvng_ref[...], KV_LORA).astype(bf16)
        kf = jnp.dot(ckv, wk_ref[...], preferred_element_type=f32)
        vt = lax.dot_general(wvt_ref[...], ckv, (((1,), (1,)), ((), ())),
                             preferred_element_type=f32)
        vt_ref[t * TM // TK, :, (t * TM) % TK:(t * TM) % TK + TM] = vt.astype(bf16)
        return qf, kf

    def mix(proj, qf, kf, t):
        rows = slice(t * TM, (t + 1) * TM)
        col = lambda off, w: proj[:, off:off + w]

        v = _layer_norm(jax.nn.gelu(col(OFF_V, SG_WIDTH)), sglng_ref[...], sglnb_ref[...], SG_WIDTH)
        mixed = []
        for c in range(TM // SG_CHUNK):
            vc = v[c * SG_CHUNK:(c + 1) * SG_CHUNK, :]
            vstack = jnp.concatenate(
                [jnp.where((lane >= g * SG_HEAD_DIM) & (lane < (g + 1) * SG_HEAD_DIM), vc, 0.0)
                 for g in range(SG_HEADS)], axis=0).astype(bf16)
            mixed.append(jnp.dot(wcat, vstack, preferred_element_type=f32) + sgbm_ref[...])
        ys = jax.nn.gelu(col(OFF_U, SG_WIDTH)) * jnp.concatenate(mixed, axis=0) * _silu(col(OFF_ZS, SG_WIDTH))
        ycs_ref[rows, CONV_WIDTH:CONV_WIDTH + SG_WIDTH] = ys.astype(bf16)

        ybuf_ref[CONV_HALO:CONV_HALO + TM, :] = col(OFF_A, CONV_WIDTH) * _sigmoid(col(OFF_GLU, CONV_WIDTH))
        first = CONV_HALO - (CONV_K - 1)
        for r in range(1, SUBLANES):
            zs_ref[r - 1, 0:TM + CONV_HALO - SUBLANES, :] = ybuf_ref[r:r + TM + CONV_HALO - SUBLANES, :]
        chunks = []
        for c0 in range(0, TM, CONV_ROWS):
            conv = jnp.zeros((CONV_ROWS, CONV_WIDTH), f32) + convb_ref[...]
            for tap in range(CONV_K):
                r, off = (tap + first) % SUBLANES, (tap + first) // SUBLANES * SUBLANES + c0
                src = ybuf_ref if r == 0 else zs_ref.at[r - 1]
                conv = conv + src[off:off + CONV_ROWS, :] * convw_ref[tap:tap + 1, :]
            chunks.append(_silu(_layer_norm(conv, clng_ref[...], clnb_ref[...], CONV_WIDTH)).astype(bf16))
        ybuf_ref[0:CONV_HALO, :] = ybuf_ref[TM:TM + CONV_HALO, :]
        yc = jnp.concatenate(chunks, axis=0)
        yc = jnp.dot(yc, pww_ref[...], preferred_element_type=f32) + pwb_ref[...]
        yc = yc * _silu(col(OFF_ZC, CONV_WIDTH))
        ycs_ref[rows, 0:CONV_WIDTH] = yc.astype(bf16)

        zm_ref[rows, :] = col(OFF_ZM, MLA_WIDTH).astype(bf16)
        rc, rsu, rsd = rc_ref[rows, :], rsu_ref[rows, :], rsd_ref[rows, :]
        qc = rc * (gq_ref[0:1, :] * LOG2E)
        qs = (rsu + rsd) * (gq_ref[1:2, :] * LOG2E)
        real = (lax.broadcasted_iota(jnp.int32, (1, HEAD_PAD), 1) < MLA_QK).astype(f32)
        for hd in range(MLA_HEADS):
            qh = qf[:, hd * HEAD_PAD:(hd + 1) * HEAD_PAD]
            ss = jnp.sum(qh * qh * real, axis=-1, keepdims=True)
            roped = qh * qc + pltpu.roll(qh, LANES - MLA_ROPE, 1) * qs
            q_ref[rows, hd * HEAD_PAD:(hd + 1) * HEAD_PAD] = (
                roped * lax.rsqrt(ss + MLA_QK * EPS)).astype(bf16)
        gk = gk_ref[...] * MLA_QK ** 0.5
        kr = col(OFF_KR, HEAD_PAD)
        ss_r = jnp.sum(kr * kr, axis=-1, keepdims=True)
        krr = _rope(kr * gk, rc, rsu, rsd)
        for hd in range(MLA_HEADS):
            kh = kf[:, hd * HEAD_PAD:(hd + 1) * HEAD_PAD]
            ss = jnp.sum(kh * kh, axis=-1, keepdims=True) + ss_r
            k_ref[rows, hd * HEAD_PAD:(hd + 1) * HEAD_PAD] = (
                (kh * gk + krr) * lax.rsqrt(ss + MLA_QK * EPS)).astype(bf16)

    trow = lax.broadcasted_iota(jnp.int32, (SG_CHUNK, SG_CHUNK), 0)
    tcol = lax.broadcasted_iota(jnp.int32, (SG_CHUNK, SG_CHUNK), 1)
    wcat = jnp.concatenate(
        [jnp.where(tcol <= trow, sgw_ref[g], 0.0) for g in range(SG_HEADS)], axis=1).astype(bf16)
    lane = lax.broadcasted_iota(jnp.int32, (SG_CHUNK, SG_WIDTH), 1)

    @pl.when(step % steps_per_batch == 0)
    def _():
        ybuf_ref[0:CONV_HALO, :] = jnp.zeros((CONV_HALO, CONV_WIDTH), f32)

    @pl.when(step == 0)
    def _():
        proj0_ref[...] = project(tile_rows(0))

    projs = {0: proj0_ref, 1: project(tile_rows(1))}
    ups = {0: up_project(proj0_ref, 0)}
    for t in range(FRONT_TILES):
        if t + 1 < FRONT_TILES:
            ups[t + 1] = up_project(projs[t + 1], t + 1)
        if t + 2 < FRONT_TILES:
            projs[t + 2] = project(tile_rows(t + 2))
        mix(projs.pop(t), *ups.pop(t), t)
    proj0_ref[...] = project(hnext_ref[...])


def _col_max8(s):
    m8 = jnp.max(s.reshape(s.shape[0] // SUBLANES, SUBLANES, s.shape[1]), axis=0)
    for shift in (4, 2, 1):
        m8 = jnp.maximum(m8, pltpu.roll(m8, shift, 0))
    return m8


def _rows8(x, fn, stat8):
    x3 = x.reshape(x.shape[0] // SUBLANES, SUBLANES, x.shape[1])
    return fn(x3, stat8[None]).reshape(x.shape)


def _attn_kernel(q_ref, k_ref, vt_ref, o_ref, sa_ref, sb_ref, maxa_ref, maxb_ref, m_ref, acc_ref):
    f32, bf16 = jnp.float32, jnp.bfloat16
    qi = pl.program_id(1)
    n_full = qi // (TK // TQ)
    causal = (lax.broadcasted_iota(jnp.int32, (TQ, TQ), 0)
              <= lax.broadcasted_iota(jnp.int32, (TQ, TQ), 1))
    ones_rows = jnp.ones((ACC_ROWS - MLA_V, TK), bf16)
    heads = range(MLA_HEADS)
    hslab = lambda hh: slice(hh * HEAD_PAD, (hh + 1) * HEAD_PAD)
    vslab = lambda hh: slice(hh * MLA_V, (hh + 1) * MLA_V)

    def scores(j, s_ref, max_ref, hh):
        kb = k_ref[pl.ds(pl.multiple_of(j * TK, TK), TK), hslab(hh)]
        s = lax.dot_general(kb, q_ref[:, hslab(hh)], (((1,), (1,)), ((), ())),
                            preferred_element_type=f32)
        s_ref[hh] = s
        max_ref[hh] = _col_max8(s)

    def consume(j, s_ref, max_ref, hh, mode):
        if mode == "full":
            s, blk_max = s_ref[hh], max_ref[hh]
        elif mode == "diag_short":
            s = jnp.where(causal, s_ref[hh, 0:TQ, :], NEG_BIG)
            blk_max = _col_max8(s)
        else:
            s = jnp.concatenate([s_ref[hh, 0:TQ, :],
                                 jnp.where(causal, s_ref[hh, TQ:TK, :], NEG_BIG)], axis=0)
            blk_max = _col_max8(s)
        keys = s.shape[0]
        m = m_ref[hh]
        m_new = jnp.maximum(m, blk_max)
        alpha = jnp.exp2(m - m_new)
        p = _rows8(s, lambda x, st: jnp.exp2(x - st), m_new)
        m_ref[hh] = m_new
        lhs = jnp.concatenate([vt_ref[j, vslab(hh), 0:keys], ones_rows[:, 0:keys]], axis=0)
        pv = jnp.dot(lhs, p.astype(bf16), preferred_element_type=f32)
        acc_ref[hh] = _rows8(acc_ref[hh], lambda x, st: x * st, alpha) + pv

    def half_step(j, cur, nxt):
        for hh in heads:
            scores(j + 1, nxt[0], nxt[1], hh)
            consume(j, cur[0], cur[1], hh, "full")

    def diagonal(buf):
        @pl.when(qi % (TK // TQ) == 0)
        def _():
            for hh in heads:
                consume(n_full, buf[0], buf[1], hh, "diag_short")

        @pl.when(qi % (TK // TQ) == 1)
        def _():
            for hh in heads:
                consume(n_full, buf[0], buf[1], hh, "diag_long")

    buf_a, buf_b = (sa_ref, maxa_ref), (sb_ref, maxb_ref)
    m_ref[...] = jnp.full(m_ref.shape, NEG_BIG, f32)
    acc_ref[...] = jnp.zeros(acc_ref.shape, f32)
    for hh in heads:
        scores(0, sa_ref, maxa_ref, hh)

    def pair(j):
        half_step(j, buf_a, buf_b)
        half_step(j + 1, buf_b, buf_a)

    def two_pairs(jj, carry):
        pair(4 * jj)
        pair(4 * jj + 2)
        return carry

    lax.fori_loop(0, n_full // 4, two_pairs, 0)
    left = n_full % 4

    @pl.when(left >= 2)
    def _():
        pair(n_full - left)

    @pl.when(left % 2 == 1)
    def _():
        half_step(n_full - 1, buf_a, buf_b)
        diagonal(buf_b)

    @pl.when(left % 2 == 0)
    def _():
        diagonal(buf_a)

    out_t = jnp.concatenate(
        [_rows8(acc_ref[hh][:MLA_V], lambda x, st: x / st, acc_ref[hh][MLA_V:MLA_V + SUBLANES])
         for hh in heads], axis=0)
    o_ref[...] = out_t.T.astype(bf16)


def _back_kernel(x_ref, ycs_ref, ot_ref, zm_ref, bgc_ref, bgm_ref, bgs_ref, wout_ref, *rest):
    f32, bf16 = jnp.float32, jnp.bfloat16
    ym = ot_ref[...].astype(f32) * _silu(zm_ref[...].astype(f32))
    ycs = ycs_ref[...].astype(f32)
    y = jnp.concatenate(
        [_rms(ycs[:, 0:CONV_WIDTH], bgc_ref[...], CONV_WIDTH).astype(bf16),
         _rms(ym, bgm_ref[...], MLA_WIDTH).astype(bf16),
         _rms(ycs[:, CONV_WIDTH:], bgs_ref[...], SG_WIDTH).astype(bf16)], axis=1)
    x_new = x_ref[...] + jnp.dot(y, wout_ref[...], preferred_element_type=f32)
    if len(rest) == 1:
        rest[0][...] = x_new
    else:
        next_g_ref, out_ref, next_h_ref = rest
        out_ref[...] = x_new
        next_h_ref[...] = _rms(x_new, next_g_ref[...], D_MODEL).astype(bf16)


REGROUP_ROWS = 256


def _regroup_kernel(w_ref, tail_ref, o_ref):
    bf16 = jnp.bfloat16
    kr0 = OFF_KR
    o_ref[:, 0:OFF_KR] = w_ref[:, 0:OFF_KR].astype(bf16)
    lane = lax.broadcasted_iota(jnp.int32, (REGROUP_ROWS, LANES), 1)
    slabs = [w_ref[:, kr0 + j * LANES:kr0 + (j + 1) * LANES]
             for j in range((IN_COLS_PAD - OFF_ZM) // LANES)] + [tail_ref[...]]
    o_ref[:, OFF_KR:OFF_ZM] = pltpu.roll(
        jnp.where(lane < MLA_ROPE, slabs[0], 0.0), MLA_NOPE, 1).astype(bf16)
    for j in range(len(slabs) - 1):
        shifted = jnp.where(lane < LANES - MLA_ROPE, pltpu.roll(slabs[j], LANES - MLA_ROPE, 1),
                            pltpu.roll(slabs[j + 1], LANES - MLA_ROPE, 1))
        o_ref[:, OFF_ZM + j * LANES:OFF_ZM + (j + 1) * LANES] = shifted.astype(bf16)


def _regroup_w_in(w_in):
    depth = w_in.shape[0]
    cols = w_in.shape[2]
    whole = cols // LANES * LANES
    tail = jnp.pad(w_in[:, :, whole:], ((0, 0), (0, 0), (0, LANES - (cols - whole))))
    return pl.pallas_call(
        _regroup_kernel,
        grid=(depth, D_MODEL // REGROUP_ROWS),
        in_specs=[pl.BlockSpec((None, REGROUP_ROWS, cols), lambda l, i: (l, i, 0)),
                  pl.BlockSpec((None, REGROUP_ROWS, LANES), lambda l, i: (l, i, 0))],
        out_specs=pl.BlockSpec((None, REGROUP_ROWS, IN_COLS_PAD), lambda l, i: (l, i, 0)),
        out_shape=jax.ShapeDtypeStruct((depth, D_MODEL, IN_COLS_PAD), jnp.bfloat16),
        compiler_params=pltpu.CompilerParams(dimension_semantics=("arbitrary", "arbitrary"),
                                             vmem_limit_bytes=VMEM_LIMIT),
        name="regroup",
    )(w_in, tail)


def _rope_lane_tables(seq):
    half = MLA_ROPE // 2
    inv_freq = ROPE_THETA ** (-jnp.arange(half, dtype=jnp.float32) / half)
    per_row = LANES // half
    pos = (jnp.arange(seq // per_row, dtype=jnp.int32)[:, None] * per_row
           + jnp.arange(LANES, dtype=jnp.int32)[None, :] // half).astype(jnp.float32)
    ang = pos * jnp.tile(inv_freq, per_row)[None, :]
    cos, sin = lax.optimization_barrier((jnp.cos(ang), jnp.sin(ang)))
    cos, sin = cos.reshape(seq, half), sin.reshape(seq, half)
    ones = jnp.ones((seq, MLA_NOPE), jnp.float32)
    zpad = jnp.zeros((seq, HEAD_PAD - MLA_QK), jnp.float32)
    zn = jnp.zeros((seq, MLA_NOPE), jnp.float32)
    zh = jnp.zeros((seq, half), jnp.float32)
    c = jnp.concatenate([ones, cos, cos, zpad], axis=1)
    s_up = jnp.concatenate([zn, -sin, zh, zpad], axis=1)
    s_dn = jnp.concatenate([zn, zh, sin, zpad], axis=1)
    return c, s_up, s_dn


def _prep_params(p):
    bf16 = jnp.bfloat16
    depth = p['w_in'].shape[0]
    half = MLA_ROPE // 2
    win = _regroup_w_in(p['w_in'])
    wq = p['w_uq'].astype(bf16).reshape(depth, Q_LORA, MLA_HEADS, MLA_QK)
    wuq = jnp.concatenate([wq, wq[..., MLA_NOPE + half:], wq[..., MLA_NOPE:MLA_NOPE + half]],
                          axis=3).reshape(depth, Q_LORA, QK_PAD)
    wukv = p['w_ukv'].astype(bf16).reshape(depth, KV_LORA, MLA_HEADS, MLA_NOPE + MLA_V)
    wk = jnp.pad(wukv[..., :MLA_NOPE], ((0, 0), (0, 0), (0, 0), (0, HEAD_PAD - MLA_NOPE))
                 ).reshape(depth, KV_LORA, QK_PAD)
    wvt = jnp.swapaxes(wukv[..., MLA_NOPE:].reshape(depth, KV_LORA, MLA_WIDTH), 1, 2)
    pad_g = lambda g: jnp.pad(g, ((0, 0), (0, HEAD_PAD - g.shape[1])))[:, None, :]
    gq = p['qk_q_g']
    partner = jnp.concatenate([jnp.zeros((depth, MLA_NOPE), gq.dtype), gq[:, MLA_NOPE + half:],
                               gq[:, MLA_NOPE:MLA_NOPE + half]], axis=1)
    bng = p['branch_norm_g']
    vec = lambda a: a[:, None, :]
    return dict(
        win=win, wuq=wuq, wk=wk, wvt=wvt,
        gq=jnp.concatenate([pad_g(gq), pad_g(partner)], axis=1), gk=pad_g(p['qk_k_g']),
        sgbm=jnp.repeat(jnp.swapaxes(p['sg_b'], 1, 2), SG_HEAD_DIM, axis=2),
        pww=p['conv_pw_w'].astype(bf16), wout=p['w_out'].astype(bf16),
        conv_w=p['conv_w'], sg_w=p['sg_w'],
        conv_b=vec(p['conv_b']), conv_ln_g=vec(p['conv_ln_g']), conv_ln_b=vec(p['conv_ln_b']),
        conv_pw_b=vec(p['conv_pw_b']), q_norm_g=vec(p['q_norm_g']), kv_norm_g=vec(p['kv_norm_g']),
        sg_ln_g=vec(p['sg_ln_g']), sg_ln_b=vec(p['sg_ln_b']), norm_g=vec(p['norm_g']),
        bg_conv=vec(bng[:, :CONV_WIDTH]), bg_mla=vec(bng[:, CONV_WIDTH:CONV_WIDTH + MLA_WIDTH]),
        bg_sg=vec(bng[:, CONV_WIDTH + MLA_WIDTH:]))


def _layer(x, h, rope, w, layer, emit_next_h):
    B, S, _ = x.shape
    n_tok = B * S
    tstep = FRONT_TILES * TM
    n_steps = n_tok // tstep
    steps_per_batch = S // tstep
    bf16 = jnp.bfloat16

    def of_layer(name, which=layer):
        shape = w[name].shape[1:]
        return w[name], pl.BlockSpec((None,) + shape, lambda *_: (which,) + (0,) * len(shape),
                                     pipeline_mode=pl.Buffered(1))

    ropespec = pl.BlockSpec((tstep, HEAD_PAD), lambda s: (s % steps_per_batch, 0))
    front_in = [
        (h, pl.BlockSpec((tstep, D_MODEL), lambda s: (s, 0))),
        (h, pl.BlockSpec((TM, D_MODEL),
                         lambda s: (jnp.minimum(FRONT_TILES * (s + 1), n_tok // TM - 1), 0))),
        of_layer('norm_g'), of_layer('win'), of_layer('conv_w'), of_layer('conv_b'), of_layer('conv_ln_g'),
        of_layer('conv_ln_b'), of_layer('pww'), of_layer('conv_pw_b'), of_layer('q_norm_g'),
        of_layer('wuq'), of_layer('kv_norm_g'), of_layer('wk'), of_layer('wvt'),
        of_layer('gq'), of_layer('gk'), of_layer('sg_ln_g'), of_layer('sg_ln_b'),
        of_layer('sg_w'), of_layer('sgbm'),
        (rope[0], ropespec), (rope[1], ropespec), (rope[2], ropespec),
    ]
    otile = lambda width: pl.BlockSpec((tstep, width), lambda s: (s, 0))
    ycs, q, k, vt, zm = pl.pallas_call(
        functools.partial(_front_kernel, steps_per_batch=steps_per_batch,
                          normalise_input=h.dtype != bf16),
        grid=(n_steps,),
        in_specs=[s for _, s in front_in],
        out_specs=[otile(CONV_WIDTH + SG_WIDTH), otile(QK_PAD), otile(QK_PAD),
                   pl.BlockSpec((tstep // TK, MLA_WIDTH, TK), lambda s: (s, 0, 0)),
                   otile(MLA_WIDTH)],
        out_shape=[jax.ShapeDtypeStruct((n_tok, CONV_WIDTH + SG_WIDTH), bf16),
                   jax.ShapeDtypeStruct((n_tok, QK_PAD), bf16),
                   jax.ShapeDtypeStruct((n_tok, QK_PAD), bf16),
                   jax.ShapeDtypeStruct((n_tok // TK, MLA_WIDTH, TK), bf16),
                   jax.ShapeDtypeStruct((n_tok, MLA_WIDTH), bf16)],
        scratch_shapes=[pltpu.VMEM((TM, IN_COLS_PAD), jnp.float32),
                        pltpu.VMEM((CONV_HALO + TM, CONV_WIDTH), jnp.float32),
                        pltpu.VMEM((SUBLANES - 1, CONV_HALO + TM, CONV_WIDTH), jnp.float32)],
        compiler_params=pltpu.CompilerParams(
            dimension_semantics=("arbitrary",), vmem_limit_bytes=VMEM_LIMIT),
        name="front",
    )(*[a for a, _ in front_in])

    nq, nkv = S // TQ, S // TK
    ot = pl.pallas_call(
        _attn_kernel,
        grid=(B, nq),
        in_specs=[pl.BlockSpec((None, TQ, QK_PAD), lambda b, i: (b, i, 0)),
                  pl.BlockSpec((None, S, QK_PAD), lambda b, i: (b, 0, 0),
                               pipeline_mode=pl.Buffered(1)),
                  pl.BlockSpec((None, nkv, MLA_WIDTH, TK), lambda b, i: (b, 0, 0, 0),
                               pipeline_mode=pl.Buffered(1))],
        out_specs=pl.BlockSpec((None, TQ, MLA_WIDTH), lambda b, i: (b, i, 0)),
        out_shape=jax.ShapeDtypeStruct((B, S, MLA_WIDTH), bf16),
        scratch_shapes=[pltpu.VMEM((MLA_HEADS, TK, TQ), jnp.float32)] * 2
                       + [pltpu.VMEM((MLA_HEADS, SUBLANES, TQ), jnp.float32)] * 3
                       + [pltpu.VMEM((MLA_HEADS, ACC_ROWS, TQ), jnp.float32)],
        compiler_params=pltpu.CompilerParams(
            dimension_semantics=("arbitrary", "arbitrary"), vmem_limit_bytes=VMEM_LIMIT),
        name="attn",
    )(q.reshape(B, S, QK_PAD), k.reshape(B, S, QK_PAD), vt.reshape(B, nkv, MLA_WIDTH, TK))

    tile = lambda width: pl.BlockSpec((None, TB, width), lambda b, i: (b, i, 0))
    back_in = [(x, tile(D_MODEL)), (ycs.reshape(B, S, -1), tile(CONV_WIDTH + SG_WIDTH)),
               (ot, tile(MLA_WIDTH)), (zm.reshape(B, S, -1), tile(MLA_WIDTH)),
               of_layer('bg_conv'), of_layer('bg_mla'), of_layer('bg_sg'), of_layer('wout')]
    out_specs = [tile(D_MODEL)]
    out_shape = [jax.ShapeDtypeStruct((B, S, D_MODEL), jnp.float32)]
    if emit_next_h:
        back_in.append(of_layer('norm_g', layer + 1))
        out_specs.append(tile(D_MODEL))
        out_shape.append(jax.ShapeDtypeStruct((B, S, D_MODEL), bf16))
    outs = pl.pallas_call(
        _back_kernel,
        grid=(B, S // TB),
        in_specs=[s for _, s in back_in],
        out_specs=out_specs,
        out_shape=out_shape,
        compiler_params=pltpu.CompilerParams(
            dimension_semantics=("arbitrary", "arbitrary"), vmem_limit_bytes=VMEM_LIMIT),
        name="back",
    )(*[a for a, _ in back_in])
    return outs[0], (outs[1].reshape(n_tok, D_MODEL) if emit_next_h else None)


def kernel(x, norm_g, w_in, conv_w, conv_b, conv_ln_g, conv_ln_b, conv_pw_w, conv_pw_b,
           q_norm_g, w_uq, kv_norm_g, w_ukv, qk_q_g, qk_k_g, sg_ln_g, sg_ln_b, sg_w, sg_b,
           branch_norm_g, w_out):
    params = dict(norm_g=norm_g, w_in=w_in, conv_w=conv_w, conv_b=conv_b, conv_ln_g=conv_ln_g,
                  conv_ln_b=conv_ln_b, conv_pw_w=conv_pw_w, conv_pw_b=conv_pw_b,
                  q_norm_g=q_norm_g, w_uq=w_uq, kv_norm_g=kv_norm_g, w_ukv=w_ukv,
                  qk_q_g=qk_q_g, qk_k_g=qk_k_g, sg_ln_g=sg_ln_g, sg_ln_b=sg_ln_b, sg_w=sg_w,
                  sg_b=sg_b, branch_norm_g=branch_norm_g, w_out=w_out)
    rope = _rope_lane_tables(x.shape[1])
    depth = norm_g.shape[0]
    w = _prep_params(params)
    h = x.reshape(-1, D_MODEL)
    for layer in range(depth):
        x, h = _layer(x, h, rope, w, layer, layer + 1 < depth)
    return x
```

```python
import functools

import jax
import jax.numpy as jnp
from jax import lax
from jax.experimental import pallas as pl
from jax.experimental.pallas import tpu as pltpu

D_MODEL = 1024
CONV_WIDTH = 256
CONV_K = 31
MLA_HEADS = 8
MLA_NOPE = 64
MLA_ROPE = 32
MLA_QK = MLA_NOPE + MLA_ROPE
MLA_V = 64
MLA_WIDTH = MLA_HEADS * MLA_V
Q_LORA = 768
KV_LORA = 256
ROPE_THETA = 10000.0
SG_WIDTH = 256
SG_HEADS = 4
SG_HEAD_DIM = SG_WIDTH // SG_HEADS
SG_CHUNK = 128
EPS = 1e-6

SUBLANES = 8
LANES = 128
HEAD_PAD = LANES
QK_PAD = MLA_HEADS * HEAD_PAD

OFF_A = 0
OFF_GLU = OFF_A + CONV_WIDTH
OFF_ZC = OFF_GLU + CONV_WIDTH
OFF_CQ = OFF_ZC + CONV_WIDTH
OFF_CKV = OFF_CQ + Q_LORA
OFF_KR = OFF_CKV + KV_LORA
OFF_ZM = OFF_KR + HEAD_PAD
OFF_U = OFF_ZM + MLA_WIDTH
OFF_V = OFF_U + SG_WIDTH
OFF_ZS = OFF_V + SG_WIDTH
IN_COLS_PAD = OFF_ZS + SG_WIDTH

CONV_HALO = 32
CONV_ROWS = 64
TM = 256
TB = 512
BACK_SLOTS = 3
FRONT_TILES = 4
TQ = 256
TK = 512
BF16_ROWS = 2 * SUBLANES
ACC_ROWS = MLA_V + BF16_ROWS
NEG_BIG = -1e30
LOG2E = 1.4426950408889634
VMEM_LIMIT = 48 * 1024 * 1024

assert FRONT_TILES >= 2 and (FRONT_TILES * TM) % TK == 0 and TK % TM == 0 and TM % SG_CHUNK == 0 and TK == 2 * TQ


def _rms(x, g, width):
    ss = jnp.sum(x * x, axis=-1, keepdims=True)
    return x * lax.rsqrt(ss + width * EPS) * (g * width ** 0.5)


def _layer_norm(x, g, b, width):
    mu = jnp.sum(x, axis=-1, keepdims=True) * (1.0 / width)
    xc = x - mu
    ss = jnp.sum(xc * xc, axis=-1, keepdims=True)
    return xc * lax.rsqrt(ss + width * EPS) * (g * width ** 0.5) + b


def _sigmoid(x):
    return 0.5 * jnp.tanh(0.5 * x) + 0.5


def _silu(x):
    return x * _sigmoid(x)


def _rope(x, c, s_up, s_dn):
    return x * c + pltpu.roll(x, LANES - MLA_ROPE // 2, 1) * s_up + pltpu.roll(x, MLA_ROPE // 2, 1) * s_dn


def _front_kernel(h_ref, hnext_ref, ing_ref, win_ref, convw_ref, convb_ref, clng_ref,
                  clnb_ref, pww_ref, pwb_ref, qng_ref, wuq_ref, kvng_ref, wk_ref, wvt_ref,
                  gq_ref, gk_ref, sglng_ref, sglnb_ref, sgw_ref, sgbm_ref,
                  rc_ref, rsu_ref, rsd_ref,
                  ycs_ref, q_ref, k_ref, vt_ref, zm_ref,
                  proj0_ref, ybuf_ref, zs_ref, *, steps_per_batch, normalise_input):
    f32, bf16 = jnp.float32, jnp.bfloat16
    step = pl.program_id(0)

    def project(h_tile):
        if normalise_input:
            h_tile = _rms(h_tile, ing_ref[...], D_MODEL).astype(bf16)
        return jnp.dot(h_tile, win_ref[...], preferred_element_type=f32)

    tile_rows = lambda t: h_ref[t * TM:(t + 1) * TM, :]

    def up_project(proj, t):
        cq = _rms(proj[:, OFF_CQ:OFF_CQ + Q_LORA], qng_ref[...], Q_LORA).astype(bf16)
        qf = jnp.dot(cq, wuq_ref[...], preferred_element_type=f32)
        ckv = _rms(proj[:, OFF_CKV:OFF_CKV + KV_LORA], kvng_ref[...], KV_LORA).astype(bf16)
        kf = jnp.dot(ckv, wk_ref[...], preferred_element_type=f32)
        vt = lax.dot_general(wvt_ref[...], ckv, (((1,), (1,)), ((), ())),
                             preferred_element_type=f32)
        vt_ref[t * TM // TK, :, (t * TM) % TK:(t * TM) % TK + TM] = vt.astype(bf16)
        return qf, kf

    def mix(proj, qf, kf, t):
        rows = slice(t * TM, (t + 1) * TM)
        col = lambda off, w: proj[:, off:off + w]

        v = _layer_norm(jax.nn.gelu(col(OFF_V, SG_WIDTH)), sglng_ref[...], sglnb_ref[...], SG_WIDTH)
        mixed = []
        for c in range(TM // SG_CHUNK):
            vc = v[c * SG_CHUNK:(c + 1) * SG_CHUNK, :]
            vstack = jnp.concatenate(
                [jnp.where((lane >= g * SG_HEAD_DIM) & (lane < (g + 1) * SG_HEAD_DIM), vc, 0.0)
                 for g in range(SG_HEADS)], axis=0).astype(bf16)
            mixed.append(jnp.dot(wcat, vstack, preferred_element_type=f32) + sgbm_ref[...])
        ys = jax.nn.gelu(col(OFF_U, SG_WIDTH)) * jnp.concatenate(mixed, axis=0) * _silu(col(OFF_ZS, SG_WIDTH))
        ycs_ref[rows, CONV_WIDTH:CONV_WIDTH + SG_WIDTH] = ys.astype(bf16)

        ybuf_ref[CONV_HALO:CONV_HALO + TM, :] = col(OFF_A, CONV_WIDTH) * _sigmoid(col(OFF_GLU, CONV_WIDTH))
        first = CONV_HALO - (CONV_K - 1)
        for r in range(1, SUBLANES):
            zs_ref[r - 1, 0:TM + CONV_HALO - SUBLANES, :] = ybuf_ref[r:r + TM + CONV_HALO - SUBLANES, :]
        chunks = []
        for c0 in range(0, TM, CONV_ROWS):
            conv = jnp.zeros((CONV_ROWS, CONV_WIDTH), f32) + convb_ref[...]
            for tap in range(CONV_K):
                r, off = (tap + first) % SUBLANES, (tap + first) // SUBLANES * SUBLANES + c0
                src = ybuf_ref if r == 0 else zs_ref.at[r - 1]
                conv = conv + src[off:off + CONV_ROWS, :] * convw_ref[tap:tap + 1, :]
            chunks.append(_silu(_layer_norm(conv, clng_ref[...], clnb_ref[...], CONV_WIDTH)).astype(bf16))
        ybuf_ref[0:CONV_HALO, :] = ybuf_ref[TM:TM + CONV_HALO, :]
        yc = jnp.concatenate(chunks, axis=0)
        yc = jnp.dot(yc, pww_ref[...], preferred_element_type=f32) + pwb_ref[...]
        yc = yc * _silu(col(OFF_ZC, CONV_WIDTH))
        ycs_ref[rows, 0:CONV_WIDTH] = yc.astype(bf16)

        zm_ref[rows, :] = col(OFF_ZM, MLA_WIDTH).astype(bf16)
        rc, rsu, rsd = rc_ref[rows, :], rsu_ref[rows, :], rsd_ref[rows, :]
        qc = rc * (gq_ref[0:1, :] * LOG2E)
        qs = (rsu + rsd) * (gq_ref[1:2, :] * LOG2E)
        real = (lax.broadcasted_iota(jnp.int32, (1, HEAD_PAD), 1) < MLA_QK).astype(f32)
        for hd in range(MLA_HEADS):
            qh = qf[:, hd * HEAD_PAD:(hd + 1) * HEAD_PAD]
            ss = jnp.sum(qh * qh * real, axis=-1, keepdims=True)
            roped = qh * qc + pltpu.roll(qh, LANES - MLA_ROPE, 1) * qs
            q_ref[rows, hd * HEAD_PAD:(hd + 1) * HEAD_PAD] = (
                roped * lax.rsqrt(ss + MLA_QK * EPS)).astype(bf16)
        gk = gk_ref[...] * MLA_QK ** 0.5
        kr = col(OFF_KR, HEAD_PAD)
        ss_r = jnp.sum(kr * kr, axis=-1, keepdims=True)
        krr = _rope(kr * gk, rc, rsu, rsd)
        for hd in range(MLA_HEADS):
            kh = kf[:, hd * HEAD_PAD:(hd + 1) * HEAD_PAD]
            ss = jnp.sum(kh * kh, axis=-1, keepdims=True) + ss_r
            k_ref[rows, hd * HEAD_PAD:(hd + 1) * HEAD_PAD] = (
                (kh * gk + krr) * lax.rsqrt(ss + MLA_QK * EPS)).astype(bf16)

    trow = lax.broadcasted_iota(jnp.int32, (SG_CHUNK, SG_CHUNK), 0)
    tcol = lax.broadcasted_iota(jnp.int32, (SG_CHUNK, SG_CHUNK), 1)
    wcat = jnp.concatenate(
        [jnp.where(tcol <= trow, sgw_ref[g], 0.0) for g in range(SG_HEADS)], axis=1).astype(bf16)
    lane = lax.broadcasted_iota(jnp.int32, (SG_CHUNK, SG_WIDTH), 1)

    @pl.when(step % steps_per_batch == 0)
    def _():
        ybuf_ref[0:CONV_HALO, :] = jnp.zeros((CONV_HALO, CONV_WIDTH), f32)

    @pl.when(step == 0)
    def _():
        proj0_ref[...] = project(tile_rows(0))

    projs = {0: proj0_ref, 1: project(tile_rows(1))}
    ups = {0: up_project(proj0_ref, 0)}
    for t in range(FRONT_TILES):
        if t + 1 < FRONT_TILES:
            ups[t + 1] = up_project(projs[t + 1], t + 1)
        if t + 2 < FRONT_TILES:
            projs[t + 2] = project(tile_rows(t + 2))
        mix(projs.pop(t), *ups.pop(t), t)
    proj0_ref[...] = project(hnext_ref[...])


def _col_max8(s):
    m8 = jnp.max(s.reshape(s.shape[0] // SUBLANES, SUBLANES, s.shape[1]), axis=0)
    for shift in (4, 2, 1):
        m8 = jnp.maximum(m8, pltpu.roll(m8, shift, 0))
    return m8


def _rows8(x, fn, stat8):
    x3 = x.reshape(x.shape[0] // SUBLANES, SUBLANES, x.shape[1])
    return fn(x3, stat8[None]).reshape(x.shape)


def _attn_kernel(q_ref, k_ref, vt_ref, o_ref, sa_ref, sb_ref, maxa_ref, maxb_ref, m_ref, acc_ref):
    f32, bf16 = jnp.float32, jnp.bfloat16
    qi = pl.program_id(1)
    n_full = qi // (TK // TQ)
    causal = (lax.broadcasted_iota(jnp.int32, (TQ, TQ), 0)
              <= lax.broadcasted_iota(jnp.int32, (TQ, TQ), 1))
    ones_rows = jnp.ones((ACC_ROWS - MLA_V, TK), bf16)
    heads = range(MLA_HEADS)
    hslab = lambda hh: slice(hh * HEAD_PAD, (hh + 1) * HEAD_PAD)
    vslab = lambda hh: slice(hh * MLA_V, (hh + 1) * MLA_V)

    def scores(j, s_ref, max_ref, hh):
        kb = k_ref[pl.ds(pl.multiple_of(j * TK, TK), TK), hslab(hh)]
        s = lax.dot_general(kb, q_ref[:, hslab(hh)], (((1,), (1,)), ((), ())),
                            preferred_element_type=f32)
        s_ref[hh] = s
        max_ref[hh] = _col_max8(s)

    def consume(j, s_ref, max_ref, hh, mode):
        if mode == "full":
            s, blk_max = s_ref[hh], max_ref[hh]
        elif mode == "diag_short":
            s = jnp.where(causal, s_ref[hh, 0:TQ, :], NEG_BIG)
            blk_max = _col_max8(s)
        else:
            s = jnp.concatenate([s_ref[hh, 0:TQ, :],
                                 jnp.where(causal, s_ref[hh, TQ:TK, :], NEG_BIG)], axis=0)
            blk_max = _col_max8(s)
        keys = s.shape[0]
        m = m_ref[hh]
        m_new = jnp.maximum(m, blk_max)
        alpha = jnp.exp2(m - m_new)
        p = _rows8(s, lambda x, st: jnp.exp2(x - st), m_new)
        m_ref[hh] = m_new
        lhs = jnp.concatenate([vt_ref[j, vslab(hh), 0:keys], ones_rows[:, 0:keys]], axis=0)
        pv = jnp.dot(lhs, p.astype(bf16), preferred_element_type=f32)
        acc_ref[hh] = _rows8(acc_ref[hh], lambda x, st: x * st, alpha) + pv

    def half_step(j, cur, nxt):
        for hh in heads:
            scores(j + 1, nxt[0], nxt[1], hh)
            consume(j, cur[0], cur[1], hh, "full")

    def diagonal(buf):
        @pl.when(qi % (TK // TQ) == 0)
        def _():
            for hh in heads:
                consume(n_full, buf[0], buf[1], hh, "diag_short")

        @pl.when(qi % (TK // TQ) == 1)
        def _():
            for hh in heads:
                consume(n_full, buf[0], buf[1], hh, "diag_long")

    buf_a, buf_b = (sa_ref, maxa_ref), (sb_ref, maxb_ref)
    m_ref[...] = jnp.full(m_ref.shape, NEG_BIG, f32)
    acc_ref[...] = jnp.zeros(acc_ref.shape, f32)
    for hh in heads:
        scores(0, sa_ref, maxa_ref, hh)

    def pair(j):
        half_step(j, buf_a, buf_b)
        half_step(j + 1, buf_b, buf_a)

    def two_pairs(jj, carry):
        pair(4 * jj)
        pair(4 * jj + 2)
        return carry

    lax.fori_loop(0, n_full // 4, two_pairs, 0)
    left = n_full % 4

    @pl.when(left >= 2)
    def _():
        pair(n_full - left)

    @pl.when(left % 2 == 1)
    def _():
        half_step(n_full - 1, buf_a, buf_b)
        diagonal(buf_b)

    @pl.when(left % 2 == 0)
    def _():
        diagonal(buf_a)

    out_t = jnp.concatenate(
        [_rows8(acc_ref[hh][:MLA_V], lambda x, st: x / st, acc_ref[hh][MLA_V:MLA_V + SUBLANES])
         for hh in heads], axis=0)
    o_ref[...] = out_t.T.astype(bf16)


def _back_kernel(x_hbm, ycs_ref, ot_ref, zm_ref, bgc_ref, bgm_ref, bgs_ref, wout_ref, *rest, n_steps):
    f32, bf16 = jnp.float32, jnp.bfloat16
    *rest, xbuf_ref, sem_ref = rest
    step = pl.program_id(0) * pl.num_programs(1) + pl.program_id(1)

    def fetch(at_step):
        slot = at_step % BACK_SLOTS
        rows = pl.ds(pl.multiple_of(at_step * TB, TB), TB)
        return pltpu.make_async_copy(x_hbm.at[rows, :], xbuf_ref.at[slot], sem_ref.at[slot])

    @pl.when(step == 0)
    def _():
        for first in range(BACK_SLOTS - 1):
            fetch(first).start()

    @pl.when(step + BACK_SLOTS - 1 < n_steps)
    def _():
        fetch(step + BACK_SLOTS - 1).start()

    fetch(step).wait()
    x_ref = xbuf_ref.at[step % BACK_SLOTS]
    ym = ot_ref[...].astype(f32) * _silu(zm_ref[...].astype(f32))
    ycs = ycs_ref[...].astype(f32)
    y = jnp.concatenate(
        [_rms(ycs[:, 0:CONV_WIDTH], bgc_ref[...], CONV_WIDTH).astype(bf16),
         _rms(ym, bgm_ref[...], MLA_WIDTH).astype(bf16),
         _rms(ycs[:, CONV_WIDTH:], bgs_ref[...], SG_WIDTH).astype(bf16)], axis=1)
    x_new = x_ref[...] + jnp.dot(y, wout_ref[...], preferred_element_type=f32)
    if len(rest) == 1:
        rest[0][...] = x_new
    else:
        next_g_ref, out_ref, next_h_ref = rest
        out_ref[...] = x_new
        next_h_ref[...] = _rms(x_new, next_g_ref[...], D_MODEL).astype(bf16)


def _rope_lane_tables(seq):
    half = MLA_ROPE // 2
    inv_freq = ROPE_THETA ** (-jnp.arange(half, dtype=jnp.float32) / half)
    per_row = LANES // half
    pos = (jnp.arange(seq // per_row, dtype=jnp.int32)[:, None] * per_row
           + jnp.arange(LANES, dtype=jnp.int32)[None, :] // half).astype(jnp.float32)
    ang = pos * jnp.tile(inv_freq, per_row)[None, :]
    cos, sin = lax.optimization_barrier((jnp.cos(ang), jnp.sin(ang)))
    cos, sin = cos.reshape(seq, half), sin.reshape(seq, half)
    ones = jnp.ones((seq, MLA_NOPE), jnp.float32)
    zpad = jnp.zeros((seq, HEAD_PAD - MLA_QK), jnp.float32)
    zn = jnp.zeros((seq, MLA_NOPE), jnp.float32)
    zh = jnp.zeros((seq, half), jnp.float32)
    c = jnp.concatenate([ones, cos, cos, zpad], axis=1)
    s_up = jnp.concatenate([zn, -sin, zh, zpad], axis=1)
    s_dn = jnp.concatenate([zn, zh, sin, zpad], axis=1)
    return c, s_up, s_dn


def _prep_params(p):
    bf16 = jnp.bfloat16
    depth = p['w_in'].shape[0]
    half = MLA_ROPE // 2
    w_in = p['w_in']
    kr0 = 3 * CONV_WIDTH + Q_LORA + KV_LORA
    win = jnp.zeros((depth, D_MODEL, IN_COLS_PAD), bf16)
    win = win.at[:, :, :OFF_KR].set(w_in[:, :, :kr0].astype(bf16))
    win = win.at[:, :, OFF_KR + MLA_NOPE:OFF_KR + MLA_QK].set(w_in[:, :, kr0:kr0 + MLA_ROPE].astype(bf16))
    win = win.at[:, :, OFF_ZM:].set(w_in[:, :, kr0 + MLA_ROPE:].astype(bf16))
    wq = p['w_uq'].astype(bf16).reshape(depth, Q_LORA, MLA_HEADS, MLA_QK)
    wuq = jnp.concatenate([wq, wq[..., MLA_NOPE + half:], wq[..., MLA_NOPE:MLA_NOPE + half]],
                          axis=3).reshape(depth, Q_LORA, QK_PAD)
    wukv = p['w_ukv'].astype(bf16).reshape(depth, KV_LORA, MLA_HEADS, MLA_NOPE + MLA_V)
    wk = jnp.pad(wukv[..., :MLA_NOPE], ((0, 0), (0, 0), (0, 0), (0, HEAD_PAD - MLA_NOPE))
                 ).reshape(depth, KV_LORA, QK_PAD)
    wvt = jnp.swapaxes(wukv[..., MLA_NOPE:].reshape(depth, KV_LORA, MLA_WIDTH), 1, 2)
    pad_g = lambda g: jnp.pad(g, ((0, 0), (0, HEAD_PAD - g.shape[1])))[:, None, :]
    gq = p['qk_q_g']
    partner = jnp.concatenate([jnp.zeros((depth, MLA_NOPE), gq.dtype), gq[:, MLA_NOPE + half:],
                               gq[:, MLA_NOPE:MLA_NOPE + half]], axis=1)
    bng = p['branch_norm_g']
    vec = lambda a: a[:, None, :]
    return dict(
        win=win, wuq=wuq, wk=wk, wvt=wvt,
        gq=jnp.concatenate([pad_g(gq), pad_g(partner)], axis=1), gk=pad_g(p['qk_k_g']),
        sgbm=jnp.repeat(jnp.swapaxes(p['sg_b'], 1, 2), SG_HEAD_DIM, axis=2),
        pww=p['conv_pw_w'].astype(bf16), wout=p['w_out'].astype(bf16),
        conv_w=p['conv_w'], sg_w=p['sg_w'],
        conv_b=vec(p['conv_b']), conv_ln_g=vec(p['conv_ln_g']), conv_ln_b=vec(p['conv_ln_b']),
        conv_pw_b=vec(p['conv_pw_b']), q_norm_g=vec(p['q_norm_g']), kv_norm_g=vec(p['kv_norm_g']),
        sg_ln_g=vec(p['sg_ln_g']), sg_ln_b=vec(p['sg_ln_b']), norm_g=vec(p['norm_g']),
        bg_conv=vec(bng[:, :CONV_WIDTH]), bg_mla=vec(bng[:, CONV_WIDTH:CONV_WIDTH + MLA_WIDTH]),
        bg_sg=vec(bng[:, CONV_WIDTH + MLA_WIDTH:]))


def _layer(x, h, rope, w, layer, emit_next_h):
    B, S, _ = x.shape
    n_tok = B * S
    tstep = FRONT_TILES * TM
    n_steps = n_tok // tstep
    steps_per_batch = S // tstep
    bf16 = jnp.bfloat16

    def of_layer(name, which=layer):
        shape = w[name].shape[1:]
        return w[name], pl.BlockSpec((None,) + shape, lambda *_: (which,) + (0,) * len(shape),
                                     pipeline_mode=pl.Buffered(1))

    ropespec = pl.BlockSpec((tstep, HEAD_PAD), lambda s: (s % steps_per_batch, 0))
    front_in = [
        (h, pl.BlockSpec((tstep, D_MODEL), lambda s: (s, 0))),
        (h, pl.BlockSpec((TM, D_MODEL),
                         lambda s: (jnp.minimum(FRONT_TILES * (s + 1), n_tok // TM - 1), 0))),
        of_layer('norm_g'), of_layer('win'), of_layer('conv_w'), of_layer('conv_b'), of_layer('conv_ln_g'),
        of_layer('conv_ln_b'), of_layer('pww'), of_layer('conv_pw_b'), of_layer('q_norm_g'),
        of_layer('wuq'), of_layer('kv_norm_g'), of_layer('wk'), of_layer('wvt'),
        of_layer('gq'), of_layer('gk'), of_layer('sg_ln_g'), of_layer('sg_ln_b'),
        of_layer('sg_w'), of_layer('sgbm'),
        (rope[0], ropespec), (rope[1], ropespec), (rope[2], ropespec),
    ]
    otile = lambda width: pl.BlockSpec((tstep, width), lambda s: (s, 0))
    ycs, q, k, vt, zm = pl.pallas_call(
        functools.partial(_front_kernel, steps_per_batch=steps_per_batch,
                          normalise_input=h.dtype != bf16),
        grid=(n_steps,),
        in_specs=[s for _, s in front_in],
        out_specs=[otile(CONV_WIDTH + SG_WIDTH), otile(QK_PAD), otile(QK_PAD),
                   pl.BlockSpec((tstep // TK, MLA_WIDTH, TK), lambda s: (s, 0, 0)),
                   otile(MLA_WIDTH)],
        out_shape=[jax.ShapeDtypeStruct((n_tok, CONV_WIDTH + SG_WIDTH), bf16),
                   jax.ShapeDtypeStruct((n_tok, QK_PAD), bf16),
                   jax.ShapeDtypeStruct((n_tok, QK_PAD), bf16),
                   jax.ShapeDtypeStruct((n_tok // TK, MLA_WIDTH, TK), bf16),
                   jax.ShapeDtypeStruct((n_tok, MLA_WIDTH), bf16)],
        scratch_shapes=[pltpu.VMEM((TM, IN_COLS_PAD), jnp.float32),
                        pltpu.VMEM((CONV_HALO + TM, CONV_WIDTH), jnp.float32),
                        pltpu.VMEM((SUBLANES - 1, CONV_HALO + TM, CONV_WIDTH), jnp.float32)],
        compiler_params=pltpu.CompilerParams(
            dimension_semantics=("arbitrary",), vmem_limit_bytes=VMEM_LIMIT),
        name="front",
    )(*[a for a, _ in front_in])

    nq, nkv = S // TQ, S // TK
    ot = pl.pallas_call(
        _attn_kernel,
        grid=(B, nq),
        in_specs=[pl.BlockSpec((None, TQ, QK_PAD), lambda b, i: (b, i, 0)),
                  pl.BlockSpec((None, S, QK_PAD), lambda b, i: (b, 0, 0),
                               pipeline_mode=pl.Buffered(1)),
                  pl.BlockSpec((None, nkv, MLA_WIDTH, TK), lambda b, i: (b, 0, 0, 0),
                               pipeline_mode=pl.Buffered(1))],
        out_specs=pl.BlockSpec((None, TQ, MLA_WIDTH), lambda b, i: (b, i, 0)),
        out_shape=jax.ShapeDtypeStruct((B, S, MLA_WIDTH), bf16),
        scratch_shapes=[pltpu.VMEM((MLA_HEADS, TK, TQ), jnp.float32)] * 2
                       + [pltpu.VMEM((MLA_HEADS, SUBLANES, TQ), jnp.float32)] * 3
                       + [pltpu.VMEM((MLA_HEADS, ACC_ROWS, TQ), jnp.float32)],
        compiler_params=pltpu.CompilerParams(
            dimension_semantics=("arbitrary", "arbitrary"), vmem_limit_bytes=VMEM_LIMIT),
        name="attn",
    )(q.reshape(B, S, QK_PAD), k.reshape(B, S, QK_PAD), vt.reshape(B, nkv, MLA_WIDTH, TK))

    tile = lambda width: pl.BlockSpec((None, TB, width), lambda b, i: (b, i, 0))
    back_in = [(x.reshape(n_tok, D_MODEL), pl.BlockSpec(memory_space=pl.ANY)),
               (ycs.reshape(B, S, -1), tile(CONV_WIDTH + SG_WIDTH)),
               (ot, tile(MLA_WIDTH)), (zm.reshape(B, S, -1), tile(MLA_WIDTH)),
               of_layer('bg_conv'), of_layer('bg_mla'), of_layer('bg_sg'), of_layer('wout')]
    out_specs = [tile(D_MODEL)]
    out_shape = [jax.ShapeDtypeStruct((B, S, D_MODEL), jnp.float32)]
    if emit_next_h:
        back_in.append(of_layer('norm_g', layer + 1))
        out_specs.append(tile(D_MODEL))
        out_shape.append(jax.ShapeDtypeStruct((B, S, D_MODEL), bf16))
    outs = pl.pallas_call(
        functools.partial(_back_kernel, n_steps=n_tok // TB),
        grid=(B, S // TB),
        in_specs=[s for _, s in back_in],
        out_specs=out_specs,
        out_shape=out_shape,
        scratch_shapes=[pltpu.VMEM((BACK_SLOTS, TB, D_MODEL), jnp.float32),
                        pltpu.SemaphoreType.DMA((BACK_SLOTS,))],
        compiler_params=pltpu.CompilerParams(
            dimension_semantics=("arbitrary", "arbitrary"), vmem_limit_bytes=VMEM_LIMIT),
        name="back",
    )(*[a for a, _ in back_in])
    return outs[0], (outs[1].reshape(n_tok, D_MODEL) if emit_next_h else None)


def kernel(x, norm_g, w_in, conv_w, conv_b, conv_ln_g, conv_ln_b, conv_pw_w, conv_pw_b,
           q_norm_g, w_uq, kv_norm_g, w_ukv, qk_q_g, qk_k_g, sg_ln_g, sg_ln_b, sg_w, sg_b,
           branch_norm_g, w_out):
    params = dict(norm_g=norm_g, w_in=w_in, conv_w=conv_w, conv_b=conv_b, conv_ln_g=conv_ln_g,
                  conv_ln_b=conv_ln_b, conv_pw_w=conv_pw_w, conv_pw_b=conv_pw_b,
                  q_norm_g=q_norm_g, w_uq=w_uq, kv_norm_g=kv_norm_g, w_ukv=w_ukv,
                  qk_q_g=qk_q_g, qk_k_g=qk_k_g, sg_ln_g=sg_ln_g, sg_ln_b=sg_ln_b, sg_w=sg_w,
                  sg_b=sg_b, branch_norm_g=branch_norm_g, w_out=w_out)
    rope = _rope_lane_tables(x.shape[1])
    depth = norm_g.shape[0]
    w = _prep_params(params)
    h = x.reshape(-1, D_MODEL)
    for layer in range(depth):
        x, h = _layer(x, h, rope, w, layer, layer + 1 < depth)
    return x
```
